```python
import math
import jax, jax.numpy as jnp
from jax import lax
import numpy as np

D_MODEL = 2048
BATCH = 16
SEQ = 256
DEPTH = 4
DEC_BATCH = 8
DEC_SEQ = 1024
PAST_LEN = 256

GRID_W = 64
N_MIXERS = 2
N_CONV_LAYERS = (DEPTH + 1) // 2
N_ATTN_LAYERS = DEPTH // 2
HEAD_DIM = 64
N_HEADS = D_MODEL // (2 * HEAD_DIM)
ROPE_BASE = 10000.0
CONV_WIDTH = 31
N_GROUPS = 8
EXPERTS_PER_GROUP = 8
N_EXPERTS = N_GROUPS * EXPERTS_PER_GROUP
TOP_K = 2
D_EXPERT = D_MODEL // 4
MOE_BLOCK = 128
Q_BLOCK = 128
ALPHA = (2.0 * DEPTH) ** 0.25
BETA = (8.0 * DEPTH) ** -0.25
LN_EPS = 1e-5
F32 = jnp.float32

kernel_name = "hybrid_conv_diffattn_hmoe_diffusion_step"


def layer_norm(x, g, b):
    xf = x.astype(F32)
    mu = jnp.mean(xf, -1, keepdims=True)
    var = jnp.mean(jnp.square(xf - mu), -1, keepdims=True)
    return ((xf - mu) * lax.rsqrt(var + LN_EPS) * g.astype(F32) + b.astype(F32)).astype(x.dtype)


def rms_norm(x, g):
    xf = x.astype(F32)
    return (xf * lax.rsqrt(jnp.mean(xf * xf, -1, keepdims=True) + LN_EPS) * g.astype(F32)).astype(x.dtype)


def modulation(cond, w_mod, b_mod):
    m = jax.nn.silu(cond) @ w_mod + b_mod
    return jnp.split(m[:, None, :], 6, axis=-1)


def axial_rope(n_tokens):
    rows = n_tokens // GRID_W
    row = jnp.repeat(jnp.arange(rows, dtype=F32), GRID_W)
    col = jnp.tile(jnp.arange(GRID_W, dtype=F32), rows)
    n_freq = HEAD_DIM // 4
    inv = ROPE_BASE ** (-jnp.arange(n_freq, dtype=F32) / n_freq)
    ang = jnp.concatenate([row[:, None] * inv, col[:, None] * inv], -1)
    return jnp.cos(ang), jnp.sin(ang)


def apply_rope(x, cos, sin):
    shp = x.shape
    xp = x.astype(F32).reshape(shp[:-1] + (HEAD_DIM // 2, 2))
    a, b = xp[..., 0], xp[..., 1]
    cs = cos[None, :, None, None, :]
    sn = sin[None, :, None, None, :]
    out = jnp.stack([a * cs - b * sn, a * sn + b * cs], -1)
    return out.reshape(shp).astype(x.dtype)


def conformer_conv(h, w_pw1, b_pw1, w_dw, b_dw, ln_g, ln_b, w_pw2, b_pw2):
    u = h @ w_pw1 + b_pw1
    a, g = jnp.split(u, 2, axis=-1)
    u = a * jax.nn.sigmoid(g)
    u = lax.conv_general_dilated(
        u, w_dw[:, None, :].astype(u.dtype), window_strides=(1,),
        padding=[(CONV_WIDTH // 2, CONV_WIDTH // 2)],
        dimension_numbers=("NWC", "WIO", "NWC"),
        feature_group_count=D_MODEL) + b_dw
    u = jax.nn.silu(layer_norm(u, ln_g, ln_b))
    return u @ w_pw2 + b_pw2


def diff_project(h, w_q, w_k, w_v):
    B, n, _ = h.shape
    q = (h @ w_q).reshape(B, n, N_HEADS, 2, HEAD_DIM)
    k = (h @ w_k).reshape(B, n, N_HEADS, 2, HEAD_DIM)
    v = (h @ w_v).reshape(B, n, N_HEADS, 2 * HEAD_DIM)
    return q, k, v


def diff_attn_core(q, k, v, lam):
    B, Nq = q.shape[0], q.shape[1]
    nb = Nq // Q_BLOCK
    qb = q.reshape(B, nb, Q_BLOCK, N_HEADS, 2, HEAD_DIM).transpose(1, 0, 2, 3, 4, 5)
    scale = HEAD_DIM ** -0.5

    def block(qblk):
        s = jnp.einsum("bqhjd,bkhjd->bhjqk", qblk, k).astype(F32) * scale
        p = jax.nn.softmax(s, axis=-1)
        a = p[:, :, 0] - lam * p[:, :, 1]
        return jnp.einsum("bhqk,bkhe->bqhe", a.astype(v.dtype), v)

    o = lax.map(block, qb)
    return o.transpose(1, 0, 2, 3, 4).reshape(B, Nq, N_HEADS, 2 * HEAD_DIM)


def diff_output(o, lam_init, subln_g, w_o):
    B, n = o.shape[0], o.shape[1]
    o = rms_norm(o, subln_g) * (1.0 - lam_init)
    return o.reshape(B, n, D_MODEL) @ w_o


def hier_moe(h, w_rg, b_rg, w_re, b_re, w_gate, w_up, w_down):
    shp = h.shape
    x = h.reshape(-1, D_MODEL)
    n = x.shape[0]
    lg = (x @ w_rg).astype(F32) + b_rg
    pg = jax.nn.softmax(lg, axis=-1)
    grp = jnp.argmax(lg, axis=-1)
    pg_sel = jnp.take_along_axis(pg, grp[:, None], axis=1)
    le = ((x @ w_re).astype(F32) + b_re).reshape(n, N_GROUPS, EXPERTS_PER_GROUP)
    le_sel = jnp.take_along_axis(le, grp[:, None, None], axis=1)[:, 0]
    top_v, top_i = lax.top_k(le_sel, TOP_K)
    wts = pg_sel * jax.nn.softmax(top_v, axis=-1)
    eid = grp[:, None] * EXPERTS_PER_GROUP + top_i
    a = n * TOP_K
    flat_e = eid.reshape(-1)
    order = jnp.argsort(flat_e)
    se = flat_e[order]
    tok = order // TOP_K
    counts = jnp.bincount(flat_e, length=N_EXPERTS)
    padded = (counts + MOE_BLOCK - 1) // MOE_BLOCK * MOE_BLOCK
    pend = jnp.cumsum(padded)
    cstart = jnp.cumsum(counts) - counts
    dest = pend[se] - padded[se] + (jnp.arange(a) - cstart[se])
    nblk = -(-a // MOE_BLOCK) + N_EXPERTS
    buf = jnp.zeros((nblk * MOE_BLOCK, D_MODEL), x.dtype).at[dest].set(x[tok])
    blk_e = jnp.minimum(jnp.searchsorted(pend, jnp.arange(nblk) * MOE_BLOCK, side="right"),
                        N_EXPERTS - 1)

    def run(args):
        xb, e = args
        return (jax.nn.silu(xb @ w_gate[e]) * (xb @ w_up[e])) @ w_down[e]

    out = lax.map(run, (buf.reshape(nblk, MOE_BLOCK, D_MODEL), blk_e)).reshape(-1, D_MODEL)
    contrib = out[dest] * wts.reshape(-1)[order][:, None].astype(x.dtype)
    y = jnp.zeros_like(x).at[tok].add(contrib)
    return y.reshape(shp)


def setup_inputs(seed: int = 0) -> dict:
    key = jax.random.key(seed)
    ks = iter(jax.random.split(key, 48))

    def nrm(shape, s):
        return jax.random.normal(next(ks), shape, F32) * s

    D, F, nA, nC = D_MODEL, D_EXPERT, N_ATTN_LAYERS, N_CONV_LAYERS
    kv_shape = (DEC_BATCH, PAST_LEN, N_HEADS, 2 * HEAD_DIM)
    return {
        "x_prompt": nrm((BATCH, SEQ, D), 1.0),
        "x_sample": nrm((DEC_BATCH, DEC_SEQ, D), 1.0),
        "c": nrm((DEC_BATCH, D), 1.0),
        "cache_k_l1": nrm(kv_shape, 1.0),
        "cache_v_l1": nrm(kv_shape, 1.0),
        "cache_k_l3": nrm(kv_shape, 1.0),
        "cache_v_l3": nrm(kv_shape, 1.0),
        "c_ctx": nrm((D,), 1.0),
        "w_mod": nrm((DEPTH, D, 6 * D), D ** -0.5),
        "b_mod": nrm((DEPTH, 6 * D), 0.01),
        "ln_mix_g": 1.0 + nrm((DEPTH, D), 0.05),
        "ln_mix_b": nrm((DEPTH, D), 0.02),
        "ln_ffn_g": 1.0 + nrm((DEPTH, D), 0.05),
        "ln_ffn_b": nrm((DEPTH, D), 0.02),
        "conv_w_pw1": nrm((nC, D, 2 * D), D ** -0.5),
        "conv_b_pw1": nrm((nC, 2 * D), 0.02),
        "conv_w_dw": nrm((nC, CONV_WIDTH, D), CONV_WIDTH ** -0.5),
        "conv_b_dw": nrm((nC, D), 0.02),
        "conv_ln_g": 1.0 + nrm((nC, D), 0.05),
        "conv_ln_b": nrm((nC, D), 0.02),
        "conv_w_pw2": nrm((nC, D, D), BETA * D ** -0.5),
        "conv_b_pw2": nrm((nC, D), 0.02),
        "attn_w_q": nrm((nA, D, D), D ** -0.5),
        "attn_w_k": nrm((nA, D, D), D ** -0.5),
        "attn_w_v": nrm((nA, D, D), D ** -0.5),
        "attn_w_o": nrm((nA, D, D), BETA * D ** -0.5),
        "attn_lambda_q1": nrm((nA, HEAD_DIM), 0.1),
        "attn_lambda_k1": nrm((nA, HEAD_DIM), 0.1),
        "attn_lambda_q2": nrm((nA, HEAD_DIM), 0.1),
        "attn_lambda_k2": nrm((nA, HEAD_DIM), 0.1),
        "attn_subln_g": 1.0 + nrm((nA, 2 * HEAD_DIM), 0.05),
        "router_w_group": nrm((DEPTH, D, N_GROUPS), D ** -0.5),
        "router_b_group": nrm((DEPTH, N_GROUPS), 0.01),
        "router_w_expert": nrm((DEPTH, D, N_EXPERTS), D ** -0.5),
        "router_b_expert": nrm((DEPTH, N_EXPERTS), 0.01),
        "moe_w_gate": nrm((DEPTH, N_EXPERTS, D, F), D ** -0.5),
        "moe_w_up": nrm((DEPTH, N_EXPERTS, D, F), D ** -0.5),
        "moe_w_down": nrm((DEPTH, N_EXPERTS, F, D), BETA * F ** -0.5),
    }


def reference(x_prompt, x_sample, c, cache_k_l1, cache_v_l1, cache_k_l3, cache_v_l3, c_ctx,
              w_mod, b_mod, ln_mix_g, ln_mix_b, ln_ffn_g, ln_ffn_b,
              conv_w_pw1, conv_b_pw1, conv_w_dw, conv_b_dw, conv_ln_g, conv_ln_b,
              conv_w_pw2, conv_b_pw2,
              attn_w_q, attn_w_k, attn_w_v, attn_w_o,
              attn_lambda_q1, attn_lambda_k1, attn_lambda_q2, attn_lambda_k2, attn_subln_g,
              router_w_group, router_b_group, router_w_expert, router_b_expert,
              moe_w_gate, moe_w_up, moe_w_down):
    caches = (cache_k_l1, cache_v_l1, cache_k_l3, cache_v_l3)
    xp, xs = x_prompt, x_sample
    Bp, n_ctx = xp.shape[0], xp.shape[1]
    Bs, n_lat = xs.shape[0], xs.shape[1]
    cos, sin = axial_rope(n_lat)
    ctx_cond = c_ctx[None, :]
    new_kv = []
    for i in range(DEPTH):
        sh_p, sc_p, g_p, sh2_p, sc2_p, g2_p = modulation(ctx_cond, w_mod[i], b_mod[i])
        sh_s, sc_s, g_s, sh2_s, sc2_s, g2_s = modulation(c, w_mod[i], b_mod[i])
        hp = xp * (1.0 + sc_p) + sh_p
        hs = xs * (1.0 + sc_s) + sh_s
        j = i // N_MIXERS
        if i % N_MIXERS == 0:
            cargs = (conv_w_pw1[j], conv_b_pw1[j], conv_w_dw[j], conv_b_dw[j],
                     conv_ln_g[j], conv_ln_b[j], conv_w_pw2[j], conv_b_pw2[j])
            mp = conformer_conv(hp, *cargs)
            ms = conformer_conv(hs, *cargs)
        else:
            lam_init = 0.8 - 0.6 * math.exp(-0.3 * i)
            lam = (jnp.exp(jnp.sum(attn_lambda_q1[j].astype(F32) * attn_lambda_k1[j].astype(F32)))
                   - jnp.exp(jnp.sum(attn_lambda_q2[j].astype(F32) * attn_lambda_k2[j].astype(F32)))
                   + lam_init)
            qp, kp, vp = diff_project(hp, attn_w_q[j], attn_w_k[j], attn_w_v[j])
            op = diff_attn_core(qp, kp, vp, lam)
            new_kv.append(kp.reshape(Bp, n_ctx, N_HEADS, 2 * HEAD_DIM))
            new_kv.append(vp)
            mp = diff_output(op, lam_init, attn_subln_g[j], attn_w_o[j])
            qs, ks_, vs = diff_project(hs, attn_w_q[j], attn_w_k[j], attn_w_v[j])
            qs = apply_rope(qs, cos, sin)
            ks_ = apply_rope(ks_, cos, sin)
            ck, cv = caches[2 * j], caches[2 * j + 1]
            n_past = ck.shape[1]
            k_all = jnp.concatenate([ks_, ck.reshape(Bs, n_past, N_HEADS, 2, HEAD_DIM).astype(ks_.dtype)], axis=1)
            v_all = jnp.concatenate([vs, cv.astype(vs.dtype)], axis=1)
            os_ = diff_attn_core(qs, k_all, v_all, lam)
            ms = diff_output(os_, lam_init, attn_subln_g[j], attn_w_o[j])
        xp = layer_norm(ALPHA * xp + g_p * mp, ln_mix_g[i], ln_mix_b[i])
        xs = layer_norm(ALPHA * xs + g_s * ms, ln_mix_g[i], ln_mix_b[i])
        margs = (router_w_group[i], router_b_group[i], router_w_expert[i], router_b_expert[i],
                 moe_w_gate[i], moe_w_up[i], moe_w_down[i])
        fp = hier_moe(xp * (1.0 + sc2_p) + sh2_p, *margs)
        fs = hier_moe(xs * (1.0 + sc2_s) + sh2_s, *margs)
        xp = layer_norm(ALPHA * xp + g2_p * fp, ln_ffn_g[i], ln_ffn_b[i])
        xs = layer_norm(ALPHA * xs + g2_s * fs, ln_ffn_g[i], ln_ffn_b[i])
    return (xp, xs, new_kv[0], new_kv[1], new_kv[2], new_kv[3])
```

```python
import functools
import math

import jax
import jax.numpy as jnp
from jax import lax
from jax.experimental import pallas as pl
from jax.experimental.pallas import tpu as pltpu

F32 = jnp.float32
BF16 = jnp.bfloat16
I32 = jnp.int32

D = 2048
DEPTH = 4
B_P, SEQ_P = 16, 256
B_S, SEQ_S = 8, 1024
PAST = 256
N_P = B_P * SEQ_P
N_S = B_S * SEQ_S
N_TOK = N_P + N_S
GRID_W = 64
HEAD_DIM = 64
N_HEADS = D // (2 * HEAD_DIM)
HEAD_W = 2 * HEAD_DIM
ROPE_BASE = 10000.0
CONV_W = 31
CONV_HALO = 16
N_GROUPS = 8
EPG = 8
N_EXPERTS = N_GROUPS * EPG
TOP_K = 2
D_EXPERT = D // 4
ALPHA = (2.0 * DEPTH) ** 0.25
LN_EPS = 1e-5
N_MOD_ROWS = 16

LANES = 128
VMEM_LIMIT = 56 * 1024 * 1024

TM = 512
TN = 512
TM_LN = 256
TC_CONV = 512
TQ = 256
TB = 128
N_SLOTS = N_TOK * TOP_K + N_EXPERTS * TB
N_BLK = N_SLOTS // TB
TR = 512
GATHER_CHUNK = 1024


def _cparams(sem):
    return pltpu.CompilerParams(dimension_semantics=sem, vmem_limit_bytes=VMEM_LIMIT)


def _mod_row(i, tm):
    t0 = i * tm
    return jnp.where(t0 < N_P, 0, (t0 - N_P) // SEQ_S + 1)


def _ln(z, g, b):
    mu = jnp.mean(z, axis=-1, keepdims=True)
    zc = z - mu
    var = jnp.mean(zc * zc, axis=-1, keepdims=True)
    return zc * lax.rsqrt(var + LN_EPS) * g + b


def _silu(x):
    return x * jax.nn.sigmoid(x)


def _bdot(a, b):
    return jnp.dot(a, b, preferred_element_type=F32)


def _mod_kernel(cond_ref, w_ref, b_ref, o_ref):
    s = _silu(cond_ref[...]).astype(BF16)
    o_ref[...] = _bdot(s, w_ref[...].astype(BF16)) + b_ref[...]


def _modulation(cond, w_mod, b_mod):
    tn = 1024
    n_out = 6 * D
    return pl.pallas_call(
        _mod_kernel,
        grid=(DEPTH, n_out // tn),
        in_specs=[
            pl.BlockSpec((N_MOD_ROWS, D), lambda l, j: (0, 0)),
            pl.BlockSpec((None, D, tn), lambda l, j: (l, 0, j)),
            pl.BlockSpec((None, 1, tn), lambda l, j: (l, 0, j)),
        ],
        out_specs=pl.BlockSpec((None, N_MOD_ROWS, tn), lambda l, j: (l, 0, j)),
        out_shape=jax.ShapeDtypeStruct((DEPTH, N_MOD_ROWS, n_out), F32),
        compiler_params=_cparams(("parallel", "parallel")),
        name="modulation",
    )(cond, w_mod, b_mod.reshape(DEPTH, 1, n_out))


def _pw1_kernel(x_ref, mod_ref, wa_ref, wg_ref, ba_ref, bg_ref, o_ref, h_ref):
    @pl.when(pl.program_id(1) == 0)
    def _():
        h_ref[...] = (x_ref[...] * (1.0 + mod_ref[1:2, :]) + mod_ref[0:1, :]).astype(BF16)

    h = h_ref[...]
    a = _bdot(h, wa_ref[...]) + ba_ref[...]
    g = _bdot(h, wg_ref[...]) + bg_ref[...]
    o_ref[...] = a * jax.nn.sigmoid(g)


def _pw1_glu(x, mods_l, w_bf, b):
    nj = D // TN
    b2 = b.reshape(1, 2 * D)
    return pl.pallas_call(
        _pw1_kernel,
        grid=(N_TOK // TM, nj),
        in_specs=[
            pl.BlockSpec((TM, D), lambda i, j: (i, 0)),
            pl.BlockSpec((None, 6, D), lambda i, j: (_mod_row(i, TM), 0, 0)),
            pl.BlockSpec((D, TN), lambda i, j: (0, j)),
            pl.BlockSpec((D, TN), lambda i, j: (0, j + nj)),
            pl.BlockSpec((1, TN), lambda i, j: (0, j)),
            pl.BlockSpec((1, TN), lambda i, j: (0, j + nj)),
        ],
        out_specs=pl.BlockSpec((TM, TN), lambda i, j: (i, j)),
        out_shape=jax.ShapeDtypeStruct((N_TOK, D), F32),
        scratch_shapes=[pltpu.VMEM((TM, D), BF16)],
        compiler_params=_cparams(("parallel", "arbitrary")),
        name="pw1_glu",
    )(x, mods_l, w_bf, w_bf, b2, b2)


def _dwconv_kernel(main_ref, top_ref, bot_ref, w_ref, b_ref, o_ref, win_ref, *, tm):
    i = pl.program_id(0)
    t0 = i * tm
    seq = jnp.where(t0 < N_P, SEQ_P, SEQ_S)
    pos = jnp.where(t0 < N_P, t0 % SEQ_P, (t0 - N_P) % SEQ_S)
    has_top = pos > 0
    has_bot = pos + tm < seq
    win_ref[0:CONV_HALO, :] = jnp.where(has_top, top_ref[...], 0.0)
    win_ref[CONV_HALO:CONV_HALO + tm, :] = main_ref[...]
    win_ref[CONV_HALO + tm:, :] = jnp.where(has_bot, bot_ref[...], 0.0)
    rs = 32
    off = CONV_HALO - CONV_W // 2
    for r0 in range(0, tm, rs):
        acc = jnp.zeros((rs, o_ref.shape[1]), F32)
        for k in range(CONV_W):
            acc = acc + win_ref[r0 + off + k:r0 + off + k + rs, :] * w_ref[k:k + 1, :]
        o_ref[r0:r0 + rs, :] = acc + b_ref[...]


def _dwconv(u, w_dw, b_dw):
    tm, tc = TM_LN, TC_CONV
    hb = tm // CONV_HALO
    n_hblk = N_TOK // CONV_HALO
    w_pad = jnp.zeros((32, D), F32).at[:CONV_W].set(w_dw)
    return pl.pallas_call(
        functools.partial(_dwconv_kernel, tm=tm),
        grid=(N_TOK // tm, D // tc),
        in_specs=[
            pl.BlockSpec((tm, tc), lambda i, c: (i, c)),
            pl.BlockSpec((CONV_HALO, tc), lambda i, c: (jnp.maximum(i * hb - 1, 0), c)),
            pl.BlockSpec((CONV_HALO, tc), lambda i, c: (jnp.minimum((i + 1) * hb, n_hblk - 1), c)),
            pl.BlockSpec((32, tc), lambda i, c: (0, c)),
            pl.BlockSpec((1, tc), lambda i, c: (0, c)),
        ],
        out_specs=pl.BlockSpec((tm, tc), lambda i, c: (i, c)),
        out_shape=jax.ShapeDtypeStruct((N_TOK, D), F32),
        scratch_shapes=[pltpu.VMEM((tm + 2 * CONV_HALO, tc), F32)],
        compiler_params=_cparams(("parallel", "parallel")),
        name="dwconv",
    )(u, u, u, w_pad, b_dw.reshape(1, D))


def _deepnorm(x, gate, m, g, b):
    return _ln(ALPHA * x + gate * m, g, b)


def _conv_out_kernel(u_ref, x_ref, mod_ref, w_ref, b_ref, cg_ref, cb_ref, lg_ref, lb_ref, o_ref):
    u = _silu(_ln(u_ref[...], cg_ref[...], cb_ref[...]))
    m = _bdot(u.astype(BF16), w_ref[...]) + b_ref[...]
    o_ref[...] = _deepnorm(x_ref[...], mod_ref[2:3, :], m, lg_ref[...], lb_ref[...])


def _attn_out_kernel(op_ref, os_ref, x_ref, mod_ref, w_ref, lg_ref, lb_ref, o_ref, *, tm):
    is_ctx = pl.program_id(0) * tm < N_P
    o = jnp.where(is_ctx, op_ref[...], os_ref[...])
    m = _bdot(o.astype(BF16), w_ref[...])
    o_ref[...] = _deepnorm(x_ref[...], mod_ref[2:3, :], m, lg_ref[...], lb_ref[...])


def _row_specs(tm):
    row = pl.BlockSpec((tm, D), lambda i: (i, 0))
    vec = pl.BlockSpec((1, D), lambda i: (0, 0))
    mod = pl.BlockSpec((None, 6, D), lambda i: (_mod_row(i, tm), 0, 0))
    return row, vec, mod


def _conv_out(u, x, mods_l, w_bf, b, conv_ln_g, conv_ln_b, ln_g, ln_b):
    tm = TM_LN
    row, vec, mod = _row_specs(tm)
    return pl.pallas_call(
        _conv_out_kernel,
        grid=(N_TOK // tm,),
        in_specs=[row, row, mod, pl.BlockSpec((D, D), lambda i: (0, 0)), vec, vec, vec, vec, vec],
        out_specs=row,
        out_shape=jax.ShapeDtypeStruct((N_TOK, D), F32),
        compiler_params=_cparams(("parallel",)),
        name="conv_out",
    )(u, x, mods_l, w_bf, b.reshape(1, D), conv_ln_g.reshape(1, D), conv_ln_b.reshape(1, D),
      ln_g.reshape(1, D), ln_b.reshape(1, D))


def _attn_out(o_ctx, o_lat, x, mods_l, w_bf, ln_g, ln_b):
    tm = TM_LN
    row, vec, mod = _row_specs(tm)
    n_ctx = N_P // tm
    return pl.pallas_call(
        functools.partial(_attn_out_kernel, tm=tm),
        grid=(N_TOK // tm,),
        in_specs=[pl.BlockSpec((tm, D), lambda i: (jnp.minimum(i, n_ctx - 1), 0)),
                  pl.BlockSpec((tm, D), lambda i: (jnp.maximum(i - n_ctx, 0), 0)),
                  row, mod, pl.BlockSpec((D, D), lambda i: (0, 0)), vec, vec],
        out_specs=row,
        out_shape=jax.ShapeDtypeStruct((N_TOK, D), F32),
        compiler_params=_cparams(("parallel",)),
        name="attn_out",
    )(o_ctx, o_lat, x, mods_l, w_bf, ln_g.reshape(1, D), ln_b.reshape(1, D))


def _qkv_kernel(x_ref, mod_ref, w_ref, cos_ref, se_ref, so_ref, o_ref, h_ref):
    i, s, j = pl.program_id(0), pl.program_id(1), pl.program_id(2)

    @pl.when((s == 0) & (j == 0))
    def _():
        h_ref[...] = (x_ref[...] * (1.0 + mod_ref[1:2, :]) + mod_ref[0:1, :]).astype(BF16)

    y = _bdot(h_ref[...], w_ref[...])
    rotate = (i * TM >= N_P) & (s < 2)

    @pl.when(rotate)
    def _():
        nxt = pltpu.roll(y, TN - 1, 1)
        prv = pltpu.roll(y, 1, 1)
        o_ref[...] = y * cos_ref[...] + nxt * se_ref[...] + prv * so_ref[...]

    @pl.when(jnp.logical_not(rotate))
    def _():
        o_ref[...] = y


def _rope_tables():
    rows = SEQ_S // GRID_W
    row = jnp.repeat(jnp.arange(rows, dtype=F32), GRID_W)
    col = jnp.tile(jnp.arange(GRID_W, dtype=F32), rows)
    n_freq = HEAD_DIM // 4
    inv = ROPE_BASE ** (-jnp.arange(n_freq, dtype=F32) / n_freq)
    ang = jnp.concatenate([row[:, None] * inv, col[:, None] * inv], -1)
    cos = jnp.repeat(jnp.cos(ang), 2, axis=-1)
    sin = jnp.repeat(jnp.sin(ang), 2, axis=-1)
    even = (jnp.arange(HEAD_DIM) % 2 == 0)[None, :]
    sin_even = jnp.where(even, -sin, 0.0)
    sin_odd = jnp.where(even, 0.0, sin)
    reps = TN // HEAD_DIM
    return tuple(jnp.tile(t, (1, reps)) for t in (cos, sin_even, sin_odd))


def _qkv_proj(x, mods_l, w_qkv_bf, rope):
    nj = D // TN
    tab = pl.BlockSpec(
        (TM, TN), lambda i, s, j: (jnp.where(i * TM >= N_P, ((i * TM - N_P) % SEQ_S) // TM, 0), 0))
    return pl.pallas_call(
        _qkv_kernel,
        grid=(N_TOK // TM, 3, nj),
        in_specs=[
            pl.BlockSpec((TM, D), lambda i, s, j: (i, 0)),
            pl.BlockSpec((None, 6, D), lambda i, s, j: (_mod_row(i, TM), 0, 0)),
            pl.BlockSpec((None, D, TN), lambda i, s, j: (s, 0, j)),
            tab, tab, tab,
        ],
        out_specs=pl.BlockSpec((None, TM, TN), lambda i, s, j: (s, i, j)),
        out_shape=jax.ShapeDtypeStruct((3, N_TOK, D), F32),
        scratch_shapes=[pltpu.VMEM((TM, D), BF16)],
        compiler_params=_cparams(("parallel", "arbitrary", "arbitrary")),
        name="qkv_proj",
    )(x, mods_l, w_qkv_bf, *rope)


def _softmax_parts(q_half, ks):
    dn = (((1,), (1,)), ((), ()))
    ss = [lax.dot_general(q_half, k, dn, preferred_element_type=F32) for k in ks]
    m = ss[0].max(axis=-1, keepdims=True)
    for s in ss[1:]:
        m = jnp.maximum(m, s.max(axis=-1, keepdims=True))
    ps = [jnp.exp(s - m) for s in ss]
    l = ps[0].sum(axis=-1, keepdims=True)
    for p in ps[1:]:
        l = l + p.sum(axis=-1, keepdims=True)
    return ps, 1.0 / l


def _attn_kernel(*refs, lam_init, has_cache):
    if has_cache:
        lam_ref, g_ref, q_ref, k_ref, v_ref, kc_ref, vc_ref, o_ref = refs
    else:
        lam_ref, g_ref, q_ref, k_ref, v_ref, o_ref = refs
    lp = lam_ref[...]
    lam = (jnp.exp(jnp.sum(lp[0:1] * lp[1:2], axis=-1, keepdims=True))
           - jnp.exp(jnp.sum(lp[2:3] * lp[3:4], axis=-1, keepdims=True)) + lam_init)
    q = q_ref[...] * (HEAD_DIM ** -0.5)
    lane = lax.broadcasted_iota(I32, q.shape, 1)
    q1 = jnp.where(lane < HEAD_DIM, q, 0.0).astype(BF16)
    q2 = jnp.where(lane >= HEAD_DIM, q, 0.0).astype(BF16)
    ks = [k_ref[...].astype(BF16)]
    vs = [v_ref[...].astype(BF16)]
    if has_cache:
        ks.append(kc_ref[...].astype(BF16))
        vs.append(vc_ref[...].astype(BF16))
    p1, r1 = _softmax_parts(q1, ks)
    p2, r2 = _softmax_parts(q2, ks)
    r2 = lam * r2
    o = None
    for a1, a2, v in zip(p1, p2, vs):
        a = (a1 * r1 - a2 * r2).astype(BF16)
        t = _bdot(a, v)
        o = t if o is None else o + t
    ms = jnp.mean(o * o, axis=-1, keepdims=True)
    o_ref[...] = o * lax.rsqrt(ms + LN_EPS) * g_ref[...] * (1.0 - lam_init)


def _attention(qkv, lam_params, subln_g, cache_k, cache_v, lam_init):
    g2 = subln_g.reshape(1, HEAD_W)
    small = [pl.BlockSpec((4, HEAD_DIM), lambda *_: (0, 0)), pl.BlockSpec((1, HEAD_W), lambda *_: (0, 0))]
    o_ctx = pl.pallas_call(
        functools.partial(_attn_kernel, lam_init=lam_init, has_cache=False),
        grid=(B_P, N_HEADS),
        in_specs=small + [
            pl.BlockSpec((None, SEQ_P, HEAD_W), lambda b, h: (0, b, h)),
            pl.BlockSpec((None, SEQ_P, HEAD_W), lambda b, h: (1, b, h)),
            pl.BlockSpec((None, SEQ_P, HEAD_W), lambda b, h: (2, b, h)),
        ],
        out_specs=pl.BlockSpec((SEQ_P, HEAD_W), lambda b, h: (b, h)),
        out_shape=jax.ShapeDtypeStruct((N_P, D), F32),
        compiler_params=_cparams(("parallel", "parallel")),
        name="attn_ctx",
    )(lam_params, g2, qkv, qkv, qkv)
    nq = SEQ_S // TQ
    q0 = N_P // TQ
    s0 = N_P // SEQ_S
    o_lat = pl.pallas_call(
        functools.partial(_attn_kernel, lam_init=lam_init, has_cache=True),
        grid=(B_S, N_HEADS, nq),
        in_specs=small + [
            pl.BlockSpec((None, TQ, HEAD_W), lambda b, h, t: (0, q0 + b * nq + t, h)),
            pl.BlockSpec((None, SEQ_S, HEAD_W), lambda b, h, t: (1, s0 + b, h)),
            pl.BlockSpec((None, SEQ_S, HEAD_W), lambda b, h, t: (2, s0 + b, h)),
            pl.BlockSpec((PAST, HEAD_W), lambda b, h, t: (b, h)),
            pl.BlockSpec((PAST, HEAD_W), lambda b, h, t: (b, h)),
        ],
        out_specs=pl.BlockSpec((TQ, HEAD_W), lambda b, h, t: (b * nq + t, h)),
        out_shape=jax.ShapeDtypeStruct((N_S, D), F32),
        compiler_params=_cparams(("parallel", "parallel", "arbitrary")),
        name="attn_lat",
    )(lam_params, g2, qkv, qkv, qkv, cache_k.reshape(B_S * PAST, D), cache_v.reshape(B_S * PAST, D))
    return o_ctx, o_lat


def _router_kernel(x_ref, mod_ref, w_ref, b_ref, h_ref, r_ref):
    h = x_ref[...] * (1.0 + mod_ref[4:5, :]) + mod_ref[3:4, :]
    h_ref[...] = h
    hh = h.astype(BF16)
    hl = (h - hh.astype(F32)).astype(BF16)
    w = w_ref[...]
    t = _bdot(hh, w)
    logit = t[:, :LANES] + t[:, LANES:] + _bdot(hl, w[:, :LANES]) + b_ref[...]
    lane = lax.broadcasted_iota(I32, logit.shape, 1).astype(F32)
    neg = -jnp.inf
    big = 4.0 * LANES
    lg = jnp.where(lane < N_GROUPS, logit, neg)
    gmax = lg.max(axis=-1, keepdims=True)
    grp = jnp.where(lg == gmax, lane, big).min(axis=-1, keepdims=True)
    pg = 1.0 / jnp.exp(lg - gmax).sum(axis=-1, keepdims=True)
    lo = N_GROUPS + grp * EPG
    le = jnp.where((lane >= lo) & (lane < lo + EPG), logit, neg)
    v1 = le.max(axis=-1, keepdims=True)
    i1 = jnp.where(le == v1, lane, big).min(axis=-1, keepdims=True)
    le2 = jnp.where(lane == i1, neg, le)
    v2 = le2.max(axis=-1, keepdims=True)
    i2 = jnp.where(le2 == v2, lane, big).min(axis=-1, keepdims=True)
    e = jnp.exp(v2 - v1)
    den = 1.0 / (1.0 + e)
    w1 = pg * den
    w2 = pg * (e * den)
    e1 = i1 - N_GROUPS
    e2 = i2 - N_GROUPS
    r_ref[...] = jnp.where(lane == 0, e1, jnp.where(lane == 1, e2, jnp.where(lane == 2, w1, jnp.where(lane == 3, w2, 0.0))))


def _router(x, mods_l, w_hl, b_r):
    tm = TM_LN
    return pl.pallas_call(
        _router_kernel,
        grid=(N_TOK // tm,),
        in_specs=[
            pl.BlockSpec((tm, D), lambda i: (i, 0)),
            pl.BlockSpec((None, 6, D), lambda i: (_mod_row(i, tm), 0, 0)),
            pl.BlockSpec((D, 2 * LANES), lambda i: (0, 0)),
            pl.BlockSpec((1, LANES), lambda i: (0, 0)),
        ],
        out_specs=[pl.BlockSpec((tm, D), lambda i: (i, 0)), pl.BlockSpec((tm, LANES), lambda i: (i, 0))],
        out_shape=[jax.ShapeDtypeStruct((N_TOK, D), F32), jax.ShapeDtypeStruct((N_TOK, LANES), F32)],
        compiler_params=_cparams(("parallel",)),
        name="router",
    )(x, mods_l, w_hl, b_r)


def _rank_kernel(r_ref, rank_ref, cnt_ref, carry_ref):
    i = pl.program_id(0)

    @pl.when(i == 0)
    def _():
        carry_ref[...] = jnp.zeros_like(carry_ref)

    r = r_ref[...]
    lane = lax.broadcasted_iota(I32, r.shape, 1)
    e1 = r[:, 0:1].astype(I32)
    e2 = r[:, 1:2].astype(I32)
    oh1 = (lane == e1).astype(F32)
    oh2 = (lane == e2).astype(F32)
    oh = oh1 + oh2
    rr = lax.broadcasted_iota(I32, (TR, TR), 0)
    cc = lax.broadcasted_iota(I32, (TR, TR), 1)
    tri = (cc < rr).astype(BF16)
    before = _bdot(tri, oh.astype(BF16)) + carry_ref[...]
    k1 = (oh1 * before).sum(axis=-1, keepdims=True)
    k2 = (oh2 * before).sum(axis=-1, keepdims=True)
    rank_ref[...] = jnp.where(lane == 0, k1, jnp.where(lane == 1, k2, 0.0))
    carry_ref[...] = carry_ref[...] + oh.sum(axis=0, keepdims=True)
    cnt_ref[...] = jnp.broadcast_to(carry_ref[...], cnt_ref.shape)


def _rank(route):
    return pl.pallas_call(
        _rank_kernel,
        grid=(N_TOK // TR,),
        in_specs=[pl.BlockSpec((TR, LANES), lambda i: (i, 0))],
        out_specs=[pl.BlockSpec((TR, LANES), lambda i: (i, 0)), pl.BlockSpec((8, LANES), lambda i: (0, 0))],
        out_shape=[jax.ShapeDtypeStruct((N_TOK, LANES), F32), jax.ShapeDtypeStruct((8, LANES), F32)],
        scratch_shapes=[pltpu.VMEM((1, LANES), F32)],
        compiler_params=_cparams(("arbitrary",)),
        name="expert_rank",
    )(route)


def _gather_kernel(idx_ref, src_ref, dst_ref, sem):
    base = pl.program_id(0) * GATHER_CHUNK

    def issue(s, carry):
        r = idx_ref[base + s]
        pltpu.make_async_copy(src_ref.at[pl.ds(r, 1)], dst_ref.at[pl.ds(base + s, 1)], sem).start()
        return carry

    lax.fori_loop(0, GATHER_CHUNK, issue, 0)
    pltpu.make_async_copy(src_ref.at[pl.ds(0, GATHER_CHUNK)], dst_ref.at[pl.ds(base, GATHER_CHUNK)], sem).wait()


def _gather_rows(src, idx):
    n = idx.shape[0]
    return pl.pallas_call(
        _gather_kernel,
        grid_spec=pltpu.PrefetchScalarGridSpec(
            num_scalar_prefetch=1,
            grid=(n // GATHER_CHUNK,),
            in_specs=[pl.BlockSpec(memory_space=pl.ANY)],
            out_specs=pl.BlockSpec(memory_space=pl.ANY),
            scratch_shapes=[pltpu.SemaphoreType.DMA(())],
        ),
        out_shape=jax.ShapeDtypeStruct((n, src.shape[1]), src.dtype),
        compiler_params=_cparams(("arbitrary",)),
        name="gather_rows",
    )(idx, src)


def _experts_kernel(blk_e_ref, n_used_ref, x_ref, wg_ref, wu_ref, wd_ref, o_ref, wg_s, wu_s, wd_s):
    i = pl.program_id(0)
    prev = blk_e_ref[jnp.maximum(i - 1, 0)]
    new_expert = (i == 0) | (blk_e_ref[i] != prev)
    used = i < n_used_ref[0]

    @pl.when(used & new_expert)
    def _():
        wg_s[...] = wg_ref[...].astype(BF16)
        wu_s[...] = wu_ref[...].astype(BF16)
        wd_s[...] = wd_ref[...].astype(BF16)

    @pl.when(used)
    def _():
        x = x_ref[...].astype(BF16)
        g = _bdot(x, wg_s[...])
        u = _bdot(x, wu_s[...])
        o_ref[...] = _bdot((_silu(g) * u).astype(BF16), wd_s[...])

    @pl.when(jnp.logical_not(used))
    def _():
        o_ref[...] = jnp.zeros_like(o_ref)


def _experts(xs, blk_e, n_used, w_gate, w_up, w_down, layer):
    def blk(i, be, nu):
        return jnp.minimum(i, nu[0] - 1)

    def w_idx(i, be, nu):
        return (layer, be[blk(i, be, nu)], 0, 0)

    return pl.pallas_call(
        _experts_kernel,
        grid_spec=pltpu.PrefetchScalarGridSpec(
            num_scalar_prefetch=2,
            grid=(N_BLK,),
            in_specs=[
                pl.BlockSpec((TB, D), lambda i, be, nu: (blk(i, be, nu), 0)),
                pl.BlockSpec((None, None, D, D_EXPERT), w_idx),
                pl.BlockSpec((None, None, D, D_EXPERT), w_idx),
                pl.BlockSpec((None, None, D_EXPERT, D), w_idx),
            ],
            out_specs=pl.BlockSpec((TB, D), lambda i, be, nu: (i, 0)),
            scratch_shapes=[pltpu.VMEM((D, D_EXPERT), BF16), pltpu.VMEM((D, D_EXPERT), BF16),
                            pltpu.VMEM((D_EXPERT, D), BF16)],
        ),
        out_shape=jax.ShapeDtypeStruct((N_SLOTS, D), F32),
        compiler_params=_cparams(("arbitrary",)),
        name="experts",
    )(blk_e, n_used, xs, w_gate, w_up, w_down)


def _ffn_out_kernel(x_ref, r0_ref, r1_ref, route_ref, mod_ref, lg_ref, lb_ref, o_ref):
    rt = route_ref[...]
    f = r0_ref[...] * rt[:, 2:3] + r1_ref[...] * rt[:, 3:4]
    z = ALPHA * x_ref[...] + mod_ref[5:6, :] * f
    o_ref[...] = _ln(z, lg_ref[...], lb_ref[...])


def _ffn_out(x, picked, route, mods_l, ln_g, ln_b):
    tm = TM_LN
    nb = N_TOK // tm
    row = pl.BlockSpec((tm, D), lambda i: (i, 0))
    vec = pl.BlockSpec((1, D), lambda i: (0, 0))
    return pl.pallas_call(
        _ffn_out_kernel,
        grid=(nb,),
        in_specs=[row, row, pl.BlockSpec((tm, D), lambda i: (i + nb, 0)),
                  pl.BlockSpec((tm, LANES), lambda i: (i, 0)),
                  pl.BlockSpec((None, 6, D), lambda i: (_mod_row(i, tm), 0, 0)), vec, vec],
        out_specs=row,
        out_shape=jax.ShapeDtypeStruct((N_TOK, D), F32),
        compiler_params=_cparams(("parallel",)),
        name="ffn_out",
    )(x, picked, picked, route, mods_l, ln_g.reshape(1, D), ln_b.reshape(1, D))


def _moe(x, mods_l, layer, w_rg, b_rg, w_re, b_re, w_gate, w_up, w_down, ln_g, ln_b):
    w_r = jnp.zeros((D, LANES), F32).at[:, :N_GROUPS].set(w_rg).at[:, N_GROUPS:N_GROUPS + N_EXPERTS].set(w_re)
    w_hi = w_r.astype(BF16)
    w_lo = (w_r - w_hi.astype(F32)).astype(BF16)
    b_r = jnp.zeros((1, LANES), F32).at[0, :N_GROUPS].set(b_rg).at[0, N_GROUPS:N_GROUPS + N_EXPERTS].set(b_re)
    h, route = _router(x, mods_l, jnp.concatenate([w_hi, w_lo], axis=1), b_r)
    rank, counts = _rank(route)
    counts = counts[0, :N_EXPERTS].astype(I32)
    padded = (counts + TB - 1) // TB * TB
    pend = jnp.cumsum(padded)
    pstart = pend - padded
    eid = route[:, :TOP_K].astype(I32)
    dest = pstart[eid] + rank[:, :TOP_K].astype(I32)
    tok = jnp.broadcast_to(jnp.arange(N_TOK, dtype=I32)[:, None], dest.shape)
    slot_tok = jnp.zeros((N_SLOTS,), I32).at[dest.reshape(-1)].set(tok.reshape(-1))
    blk_e = jnp.minimum(jnp.searchsorted(pend, jnp.arange(N_BLK, dtype=I32) * TB, side="right"),
                        N_EXPERTS - 1).astype(I32)
    n_used = (pend[-1:] // TB).astype(I32)
    xs = _gather_rows(h, slot_tok)
    ys = _experts(xs, blk_e, n_used, w_gate, w_up, w_down, layer)
    picked = _gather_rows(ys, dest.T.reshape(-1))
    return _ffn_out(x, picked, route, mods_l, ln_g, ln_b)


def kernel(x_prompt, x_sample, c, cache_k_l1, cache_v_l1, cache_k_l3, cache_v_l3, c_ctx, w_mod, b_mod, ln_mix_g, ln_mix_b, ln_ffn_g, ln_ffn_b, conv_w_pw1, conv_b_pw1, conv_w_dw, conv_b_dw, conv_ln_g, conv_ln_b, conv_w_pw2, conv_b_pw2, attn_w_q, attn_w_k, attn_w_v, attn_w_o, attn_lambda_q1, attn_lambda_k1, attn_lambda_q2, attn_lambda_k2, attn_subln_g, router_w_group, router_b_group, router_w_expert, router_b_expert, moe_w_gate, moe_w_up, moe_w_down):
    caches = ((cache_k_l1, cache_v_l1), (cache_k_l3, cache_v_l3))
    x = jnp.concatenate([x_prompt.reshape(N_P, D), x_sample.reshape(N_S, D)], axis=0)
    cond = jnp.zeros((N_MOD_ROWS, D), F32).at[0].set(c_ctx).at[1:1 + B_S].set(c)
    mods = _modulation(cond, w_mod, b_mod).reshape(DEPTH, N_MOD_ROWS, 6, D)
    rope = _rope_tables()
    new_kv = []
    for i in range(DEPTH):
        j = i // 2
        mods_l = mods[i]
        if i % 2 == 0:
            u = _pw1_glu(x, mods_l, conv_w_pw1[j].astype(BF16), conv_b_pw1[j])
            u = _dwconv(u, conv_w_dw[j], conv_b_dw[j])
            x = _conv_out(u, x, mods_l, conv_w_pw2[j].astype(BF16), conv_b_pw2[j], conv_ln_g[j], conv_ln_b[j],
                          ln_mix_g[i], ln_mix_b[i])
        else:
            lam_init = 0.8 - 0.6 * math.exp(-0.3 * i)
            w_qkv = jnp.stack([attn_w_q[j], attn_w_k[j], attn_w_v[j]]).astype(BF16)
            qkv = _qkv_proj(x, mods_l, w_qkv, rope)
            new_kv.append(qkv[1, :N_P].reshape(B_P, SEQ_P, N_HEADS, HEAD_W))
            new_kv.append(qkv[2, :N_P].reshape(B_P, SEQ_P, N_HEADS, HEAD_W))
            lam_params = jnp.stack([attn_lambda_q1[j], attn_lambda_k1[j], attn_lambda_q2[j], attn_lambda_k2[j]])
            o_ctx, o_lat = _attention(qkv, lam_params, attn_subln_g[j], caches[j][0], caches[j][1], lam_init)
            x = _attn_out(o_ctx, o_lat, x, mods_l, attn_w_o[j].astype(BF16), ln_mix_g[i], ln_mix_b[i])
        x = _moe(x, mods_l, i, router_w_group[i], router_b_group[i], router_w_expert[i], router_b_expert[i],
                 moe_w_gate, moe_w_up, moe_w_down, ln_ffn_g[i], ln_ffn_b[i])
    return (x[:N_P].reshape(B_P, SEQ_P, D), x[N_P:].reshape(B_S, SEQ_S, D),
            new_kv[0], new_kv[1], new_kv[2], new_kv[3])
```

```python
import functools
import math

import jax
import jax.numpy as jnp
from jax import lax
from jax.experimental import pallas as pl
from jax.experimental.pallas import tpu as pltpu

F32 = jnp.float32
BF16 = jnp.bfloat16
I32 = jnp.int32

D = 2048
DEPTH = 4
B_P, SEQ_P = 16, 256
B_S, SEQ_S = 8, 1024
PAST = 256
N_P = B_P * SEQ_P
N_S = B_S * SEQ_S
N_TOK = N_P + N_S
GRID_W = 64
HEAD_DIM = 64
N_HEADS = D // (2 * HEAD_DIM)
HEAD_W = 2 * HEAD_DIM
ROPE_BASE = 10000.0
CONV_W = 31
CONV_HALO = 16
N_GROUPS = 8
EPG = 8
N_EXPERTS = N_GROUPS * EPG
TOP_K = 2
D_EXPERT = D // 4
ALPHA = (2.0 * DEPTH) ** 0.25
LN_EPS = 1e-5
N_MOD_ROWS = 16

LANES = 128
SUBLANES = 8
SLAB = D // LANES
VMEM_LIMIT = 56 * 1024 * 1024

TM = 512
TN = 512
TM_LN = 256
TC_CONV = 512
TQ = 256
HEADS_CTX = 8
HEADS_LAT = 2
TB = 128
N_SLOTS = N_TOK * TOP_K + N_EXPERTS * TB
N_BLK = N_SLOTS // TB
TR = 512
GATHER_CHUNK = 1024


def _cparams(sem):
    return pltpu.CompilerParams(dimension_semantics=sem, vmem_limit_bytes=VMEM_LIMIT)


def _mod_row(i, tm):
    t0 = i * tm
    return jnp.where(t0 < N_P, 0, (t0 - N_P) // SEQ_S + 1)


def _ln(z, g, b):
    mu = jnp.mean(z, axis=-1, keepdims=True)
    zc = z - mu
    var = jnp.mean(zc * zc, axis=-1, keepdims=True)
    return zc * lax.rsqrt(var + LN_EPS) * g + b


def _silu(x):
    return x * jax.nn.sigmoid(x)


def _bdot(a, b):
    return jnp.dot(a, b, preferred_element_type=F32)


def _mod_kernel(cond_ref, w_ref, b_ref, o_ref):
    s = _silu(cond_ref[...]).astype(BF16)
    o_ref[...] = _bdot(s, w_ref[...].astype(BF16)) + b_ref[...]


def _modulation(cond, w_mod, b_mod):
    tn = 1024
    n_out = 6 * D
    return pl.pallas_call(
        _mod_kernel,
        grid=(DEPTH, n_out // tn),
        in_specs=[
            pl.BlockSpec((N_MOD_ROWS, D), lambda l, j: (0, 0)),
            pl.BlockSpec((None, D, tn), lambda l, j: (l, 0, j)),
            pl.BlockSpec((None, 1, tn), lambda l, j: (l, 0, j)),
        ],
        out_specs=pl.BlockSpec((None, N_MOD_ROWS, tn), lambda l, j: (l, 0, j)),
        out_shape=jax.ShapeDtypeStruct((DEPTH, N_MOD_ROWS, n_out), F32),
        compiler_params=_cparams(("parallel", "parallel")),
        name="modulation",
    )(cond, w_mod, b_mod.reshape(DEPTH, 1, n_out))


def _pw1_kernel(x_ref, mod_ref, wa_ref, wg_ref, ba_ref, bg_ref, o_ref, h_ref):
    @pl.when(pl.program_id(1) == 0)
    def _():
        h_ref[...] = (x_ref[...] * (1.0 + mod_ref[1:2, :]) + mod_ref[0:1, :]).astype(BF16)

    h = h_ref[...]
    a = _bdot(h, wa_ref[...]) + ba_ref[...]
    g = _bdot(h, wg_ref[...]) + bg_ref[...]
    o_ref[...] = a * jax.nn.sigmoid(g)


def _pw1_glu(x, mods_l, w_bf, b):
    nj = D // TN
    b2 = b.reshape(1, 2 * D)
    return pl.pallas_call(
        _pw1_kernel,
        grid=(N_TOK // TM, nj),
        in_specs=[
            pl.BlockSpec((TM, D), lambda i, j: (i, 0)),
            pl.BlockSpec((None, 6, D), lambda i, j: (_mod_row(i, TM), 0, 0)),
            pl.BlockSpec((D, TN), lambda i, j: (0, j)),
            pl.BlockSpec((D, TN), lambda i, j: (0, j + nj)),
            pl.BlockSpec((1, TN), lambda i, j: (0, j)),
            pl.BlockSpec((1, TN), lambda i, j: (0, j + nj)),
        ],
        out_specs=pl.BlockSpec((TM, TN), lambda i, j: (i, j)),
        out_shape=jax.ShapeDtypeStruct((N_TOK, D), F32),
        scratch_shapes=[pltpu.VMEM((TM, D), BF16)],
        compiler_params=_cparams(("parallel", "arbitrary")),
        name="pw1_glu",
    )(x, mods_l, w_bf, w_bf, b2, b2)


def _dwconv_kernel(main_ref, top_ref, bot_ref, w_ref, b_ref, o_ref, win_ref, *, tm):
    i = pl.program_id(0)
    t0 = i * tm
    seq = jnp.where(t0 < N_P, SEQ_P, SEQ_S)
    pos = jnp.where(t0 < N_P, t0 % SEQ_P, (t0 - N_P) % SEQ_S)
    has_top = pos > 0
    has_bot = pos + tm < seq
    win_ref[0, 0:CONV_HALO, :] = jnp.where(has_top, top_ref[...], 0.0)
    win_ref[0, CONV_HALO:CONV_HALO + tm, :] = main_ref[...]
    win_ref[0, CONV_HALO + tm:, :] = jnp.where(has_bot, bot_ref[...], 0.0)
    span = tm + 2 * CONV_HALO - SUBLANES
    for s in range(1, SUBLANES):
        win_ref[s, 0:span, :] = win_ref[0, s:s + span, :]
    rs = 32
    off = CONV_HALO - CONV_W // 2
    for r0 in range(0, tm, rs):
        acc = jnp.zeros((rs, o_ref.shape[1]), F32)
        for k in range(CONV_W):
            a, s = divmod(off + k, SUBLANES)
            acc = acc + win_ref[s, r0 + a * SUBLANES:r0 + a * SUBLANES + rs, :] * w_ref[k:k + 1, :]
        o_ref[r0:r0 + rs, :] = acc + b_ref[...]


def _dwconv(u, w_dw, b_dw):
    tm, tc = TM_LN, TC_CONV
    hb = tm // CONV_HALO
    n_hblk = N_TOK // CONV_HALO
    w_pad = jnp.zeros((32, D), F32).at[:CONV_W].set(w_dw)
    return pl.pallas_call(
        functools.partial(_dwconv_kernel, tm=tm),
        grid=(N_TOK // tm, D // tc),
        in_specs=[
            pl.BlockSpec((tm, tc), lambda i, c: (i, c)),
            pl.BlockSpec((CONV_HALO, tc), lambda i, c: (jnp.maximum(i * hb - 1, 0), c)),
            pl.BlockSpec((CONV_HALO, tc), lambda i, c: (jnp.minimum((i + 1) * hb, n_hblk - 1), c)),
            pl.BlockSpec((32, tc), lambda i, c: (0, c)),
            pl.BlockSpec((1, tc), lambda i, c: (0, c)),
        ],
        out_specs=pl.BlockSpec((tm, tc), lambda i, c: (i, c)),
        out_shape=jax.ShapeDtypeStruct((N_TOK, D), F32),
        scratch_shapes=[pltpu.VMEM((SUBLANES, tm + 2 * CONV_HALO, tc), F32)],
        compiler_params=_cparams(("parallel", "parallel")),
        name="dwconv",
    )(u, u, u, w_pad, b_dw.reshape(1, D))


def _deepnorm(x, gate, m, g, b):
    return _ln(ALPHA * x + gate * m, g, b)


def _conv_out_kernel(u_ref, x_ref, mod_ref, w_ref, b_ref, cg_ref, cb_ref, lg_ref, lb_ref, o_ref):
    u = _silu(_ln(u_ref[...], cg_ref[...], cb_ref[...]))
    m = _bdot(u.astype(BF16), w_ref[...]) + b_ref[...]
    o_ref[...] = _deepnorm(x_ref[...], mod_ref[2:3, :], m, lg_ref[...], lb_ref[...])


def _attn_out_kernel(op_ref, os_ref, x_ref, mod_ref, w_ref, lg_ref, lb_ref, o_ref, *, tm):
    is_ctx = pl.program_id(0) * tm < N_P
    o = jnp.where(is_ctx, op_ref[...], os_ref[...])
    m = _bdot(o.astype(BF16), w_ref[...])
    o_ref[...] = _deepnorm(x_ref[...], mod_ref[2:3, :], m, lg_ref[...], lb_ref[...])


def _row_specs(tm):
    row = pl.BlockSpec((tm, D), lambda i: (i, 0))
    vec = pl.BlockSpec((1, D), lambda i: (0, 0))
    mod = pl.BlockSpec((None, 6, D), lambda i: (_mod_row(i, tm), 0, 0))
    return row, vec, mod


def _conv_out(u, x, mods_l, w_bf, b, conv_ln_g, conv_ln_b, ln_g, ln_b):
    tm = TM_LN
    row, vec, mod = _row_specs(tm)
    return pl.pallas_call(
        _conv_out_kernel,
        grid=(N_TOK // tm,),
        in_specs=[row, row, mod, pl.BlockSpec((D, D), lambda i: (0, 0)), vec, vec, vec, vec, vec],
        out_specs=row,
        out_shape=jax.ShapeDtypeStruct((N_TOK, D), F32),
        compiler_params=_cparams(("parallel",)),
        name="conv_out",
    )(u, x, mods_l, w_bf, b.reshape(1, D), conv_ln_g.reshape(1, D), conv_ln_b.reshape(1, D),
      ln_g.reshape(1, D), ln_b.reshape(1, D))


def _attn_out(o_ctx, o_lat, x, mods_l, w_bf, ln_g, ln_b):
    tm = TM_LN
    row, vec, mod = _row_specs(tm)
    n_ctx = N_P // tm
    return pl.pallas_call(
        functools.partial(_attn_out_kernel, tm=tm),
        grid=(N_TOK // tm,),
        in_specs=[pl.BlockSpec((tm, D), lambda i: (jnp.minimum(i, n_ctx - 1), 0)),
                  pl.BlockSpec((tm, D), lambda i: (jnp.maximum(i - n_ctx, 0), 0)),
                  row, mod, pl.BlockSpec((D, D), lambda i: (0, 0)), vec, vec],
        out_specs=row,
        out_shape=jax.ShapeDtypeStruct((N_TOK, D), F32),
        compiler_params=_cparams(("parallel",)),
        name="attn_out",
    )(o_ctx, o_lat, x, mods_l, w_bf, ln_g.reshape(1, D), ln_b.reshape(1, D))


def _qkv_kernel(x_ref, mod_ref, w_ref, cos_ref, se_ref, so_ref, o_ref, h_ref):
    i, s, j = pl.program_id(0), pl.program_id(1), pl.program_id(2)

    @pl.when((s == 0) & (j == 0))
    def _():
        h_ref[...] = (x_ref[...] * (1.0 + mod_ref[1:2, :]) + mod_ref[0:1, :]).astype(BF16)

    y = _bdot(h_ref[...], w_ref[...])
    rotate = (i * TM >= N_P) & (s < 2)

    @pl.when(rotate)
    def _():
        nxt = pltpu.roll(y, TN - 1, 1)
        prv = pltpu.roll(y, 1, 1)
        o_ref[...] = y * cos_ref[...] + nxt * se_ref[...] + prv * so_ref[...]

    @pl.when(jnp.logical_not(rotate))
    def _():
        o_ref[...] = y


def _rope_tables():
    rows = SEQ_S // GRID_W
    row = jnp.repeat(jnp.arange(rows, dtype=F32), GRID_W)
    col = jnp.tile(jnp.arange(GRID_W, dtype=F32), rows)
    n_freq = HEAD_DIM // 4
    inv = ROPE_BASE ** (-jnp.arange(n_freq, dtype=F32) / n_freq)
    ang = jnp.concatenate([row[:, None] * inv, col[:, None] * inv], -1)
    cos = jnp.repeat(jnp.cos(ang), 2, axis=-1)
    sin = jnp.repeat(jnp.sin(ang), 2, axis=-1)
    even = (jnp.arange(HEAD_DIM) % 2 == 0)[None, :]
    sin_even = jnp.where(even, -sin, 0.0)
    sin_odd = jnp.where(even, 0.0, sin)
    reps = TN // HEAD_DIM
    return tuple(jnp.tile(t, (1, reps)) for t in (cos, sin_even, sin_odd))


def _qkv_proj(x, mods_l, w_qkv_bf, rope):
    nj = D // TN
    tab = pl.BlockSpec(
        (TM, TN), lambda i, s, j: (jnp.where(i * TM >= N_P, ((i * TM - N_P) % SEQ_S) // TM, 0), 0))
    return pl.pallas_call(
        _qkv_kernel,
        grid=(N_TOK // TM, 3, nj),
        in_specs=[
            pl.BlockSpec((TM, D), lambda i, s, j: (i, 0)),
            pl.BlockSpec((None, 6, D), lambda i, s, j: (_mod_row(i, TM), 0, 0)),
            pl.BlockSpec((None, D, TN), lambda i, s, j: (s, 0, j)),
            tab, tab, tab,
        ],
        out_specs=pl.BlockSpec((None, TM, TN), lambda i, s, j: (s, i, j)),
        out_shape=jax.ShapeDtypeStruct((3, N_TOK, D), F32),
        scratch_shapes=[pltpu.VMEM((TM, D), BF16)],
        compiler_params=_cparams(("parallel", "arbitrary", "arbitrary")),
        name="qkv_proj",
    )(x, mods_l, w_qkv_bf, *rope)


def _softmax_parts(q_half, ks):
    dn = (((1,), (1,)), ((), ()))
    ss = [lax.dot_general(q_half, k, dn, preferred_element_type=F32) for k in ks]
    m = ss[0].max(axis=-1, keepdims=True)
    for s in ss[1:]:
        m = jnp.maximum(m, s.max(axis=-1, keepdims=True))
    ps = [jnp.exp(s - m) for s in ss]
    l = ps[0].sum(axis=-1, keepdims=True)
    for p in ps[1:]:
        l = l + p.sum(axis=-1, keepdims=True)
    return ps, 1.0 / l


def _attn_kernel(*refs, lam_init, has_cache, heads):
    if has_cache:
        lam_ref, g_ref, q_ref, k_ref, v_ref, kc_ref, vc_ref, o_ref = refs
    else:
        lam_ref, g_ref, q_ref, k_ref, v_ref, o_ref = refs
    lp = lam_ref[...]
    lam = (jnp.exp(jnp.sum(lp[0:1] * lp[1:2], axis=-1, keepdims=True))
           - jnp.exp(jnp.sum(lp[2:3] * lp[3:4], axis=-1, keepdims=True)) + lam_init)
    lane = lax.broadcasted_iota(I32, (q_ref.shape[0], HEAD_W), 1)
    for hd in range(heads):
        cols = slice(hd * HEAD_W, (hd + 1) * HEAD_W)
        q = q_ref[:, cols] * (HEAD_DIM ** -0.5)
        q1 = jnp.where(lane < HEAD_DIM, q, 0.0).astype(BF16)
        q2 = jnp.where(lane >= HEAD_DIM, q, 0.0).astype(BF16)
        ks = [k_ref[:, cols].astype(BF16)]
        vs = [v_ref[:, cols].astype(BF16)]
        if has_cache:
            ks.append(kc_ref[:, cols].astype(BF16))
            vs.append(vc_ref[:, cols].astype(BF16))
        p1, r1 = _softmax_parts(q1, ks)
        p2, r2 = _softmax_parts(q2, ks)
        r2 = lam * r2
        o = None
        for a1, a2, v in zip(p1, p2, vs):
            a = (a1 * r1 - a2 * r2).astype(BF16)
            t = _bdot(a, v)
            o = t if o is None else o + t
        ms = jnp.mean(o * o, axis=-1, keepdims=True)
        o_ref[:, cols] = o * lax.rsqrt(ms + LN_EPS) * g_ref[...] * (1.0 - lam_init)


def _attention(qkv, lam_params, subln_g, cache_k, cache_v, lam_init):
    g2 = subln_g.reshape(1, HEAD_W)
    small = [pl.BlockSpec((4, HEAD_DIM), lambda *_: (0, 0)), pl.BlockSpec((1, HEAD_W), lambda *_: (0, 0))]
    wc = HEADS_CTX * HEAD_W
    o_ctx = pl.pallas_call(
        functools.partial(_attn_kernel, lam_init=lam_init, has_cache=False, heads=HEADS_CTX),
        grid=(B_P, N_HEADS // HEADS_CTX),
        in_specs=small + [
            pl.BlockSpec((None, SEQ_P, wc), lambda b, h: (0, b, h)),
            pl.BlockSpec((None, SEQ_P, wc), lambda b, h: (1, b, h)),
            pl.BlockSpec((None, SEQ_P, wc), lambda b, h: (2, b, h)),
        ],
        out_specs=pl.BlockSpec((SEQ_P, wc), lambda b, h: (b, h)),
        out_shape=jax.ShapeDtypeStruct((N_P, D), F32),
        compiler_params=_cparams(("parallel", "parallel")),
        name="attn_ctx",
    )(lam_params, g2, qkv, qkv, qkv)
    nq = SEQ_S // TQ
    q0 = N_P // TQ
    s0 = N_P // SEQ_S
    wl = HEADS_LAT * HEAD_W
    o_lat = pl.pallas_call(
        functools.partial(_attn_kernel, lam_init=lam_init, has_cache=True, heads=HEADS_LAT),
        grid=(B_S, N_HEADS // HEADS_LAT, nq),
        in_specs=small + [
            pl.BlockSpec((None, TQ, wl), lambda b, h, t: (0, q0 + b * nq + t, h)),
            pl.BlockSpec((None, SEQ_S, wl), lambda b, h, t: (1, s0 + b, h)),
            pl.BlockSpec((None, SEQ_S, wl), lambda b, h, t: (2, s0 + b, h)),
            pl.BlockSpec((PAST, wl), lambda b, h, t: (b, h)),
            pl.BlockSpec((PAST, wl), lambda b, h, t: (b, h)),
        ],
        out_specs=pl.BlockSpec((TQ, wl), lambda b, h, t: (b * nq + t, h)),
        out_shape=jax.ShapeDtypeStruct((N_S, D), F32),
        compiler_params=_cparams(("parallel", "parallel", "arbitrary")),
        name="attn_lat",
    )(lam_params, g2, qkv, qkv, qkv, cache_k.reshape(B_S * PAST, D), cache_v.reshape(B_S * PAST, D))
    return o_ctx, o_lat


def _router_kernel(x_ref, mod_ref, w_ref, b_ref, h_ref, r_ref, cnt_ref, carry_ref, *, tm):
    @pl.when(pl.program_id(0) == 0)
    def _():
        carry_ref[...] = jnp.zeros_like(carry_ref)

    h = x_ref[...] * (1.0 + mod_ref[4:5, :]) + mod_ref[3:4, :]
    for c in range(SLAB):
        h_ref[pl.ds(c, tm, stride=SLAB), :] = h[:, c * LANES:(c + 1) * LANES]
    hh = h.astype(BF16)
    hl = (h - hh.astype(F32)).astype(BF16)
    w = w_ref[...]
    t = _bdot(hh, w)
    logit = t[:, :LANES] + t[:, LANES:] + _bdot(hl, w[:, :LANES]) + b_ref[...]
    lane = lax.broadcasted_iota(I32, logit.shape, 1).astype(F32)
    neg = -jnp.inf
    big = 4.0 * LANES
    lg = jnp.where(lane < N_GROUPS, logit, neg)
    gmax = lg.max(axis=-1, keepdims=True)
    grp = jnp.where(lg == gmax, lane, big).min(axis=-1, keepdims=True)
    pg = 1.0 / jnp.exp(lg - gmax).sum(axis=-1, keepdims=True)
    lo = N_GROUPS + grp * EPG
    le = jnp.where((lane >= lo) & (lane < lo + EPG), logit, neg)
    v1 = le.max(axis=-1, keepdims=True)
    i1 = jnp.where(le == v1, lane, big).min(axis=-1, keepdims=True)
    le2 = jnp.where(lane == i1, neg, le)
    v2 = le2.max(axis=-1, keepdims=True)
    i2 = jnp.where(le2 == v2, lane, big).min(axis=-1, keepdims=True)
    e = jnp.exp(v2 - v1)
    den = 1.0 / (1.0 + e)
    w1 = pg * den
    w2 = pg * (e * den)
    e1 = i1 - N_GROUPS
    e2 = i2 - N_GROUPS
    r_ref[...] = jnp.where(lane == 0, e1, jnp.where(lane == 1, e2, jnp.where(lane == 2, w1, jnp.where(lane == 3, w2, 0.0))))
    picked = jnp.where(lane == e1, 1.0, jnp.where(lane == e2, 1.0, 0.0))
    carry_ref[...] = carry_ref[...] + picked.sum(axis=0, keepdims=True)
    cnt_ref[...] = jnp.broadcast_to(carry_ref[...], cnt_ref.shape)


def _router(x, mods_l, w_hl, b_r):
    tm = TM_LN
    return pl.pallas_call(
        functools.partial(_router_kernel, tm=tm),
        grid=(N_TOK // tm,),
        in_specs=[
            pl.BlockSpec((tm, D), lambda i: (i, 0)),
            pl.BlockSpec((None, 6, D), lambda i: (_mod_row(i, tm), 0, 0)),
            pl.BlockSpec((D, 2 * LANES), lambda i: (0, 0)),
            pl.BlockSpec((1, LANES), lambda i: (0, 0)),
        ],
        out_specs=[pl.BlockSpec((tm * SLAB, LANES), lambda i: (i, 0)),
                   pl.BlockSpec((tm, LANES), lambda i: (i, 0)),
                   pl.BlockSpec((SUBLANES, LANES), lambda i: (0, 0))],
        out_shape=[jax.ShapeDtypeStruct((N_TOK * SLAB, LANES), F32),
                   jax.ShapeDtypeStruct((N_TOK, LANES), F32),
                   jax.ShapeDtypeStruct((SUBLANES, LANES), F32)],
        scratch_shapes=[pltpu.VMEM((1, LANES), F32)],
        compiler_params=_cparams(("arbitrary",)),
        name="router",
    )(x, mods_l, w_hl, b_r)


def _slot_kernel(r_ref, start_ref, slot_ref, carry_ref):
    @pl.when(pl.program_id(0) == 0)
    def _():
        carry_ref[...] = jnp.zeros_like(carry_ref)

    r = r_ref[...]
    lane = lax.broadcasted_iota(I32, r.shape, 1)
    e1 = r[:, 0:1].astype(I32)
    e2 = r[:, 1:2].astype(I32)
    oh1 = (lane == e1).astype(F32)
    oh2 = (lane == e2).astype(F32)
    oh = oh1 + oh2
    rr = lax.broadcasted_iota(I32, (TR, TR), 0)
    cc = lax.broadcasted_iota(I32, (TR, TR), 1)
    tri = (cc < rr).astype(BF16)
    before = _bdot(tri, oh.astype(BF16)) + carry_ref[...] + start_ref[...]
    k1 = (oh1 * before).sum(axis=-1, keepdims=True)
    k2 = (oh2 * before).sum(axis=-1, keepdims=True)
    slot_ref[...] = jnp.where(lane == 0, k1, jnp.where(lane == 1, k2, 0.0))
    carry_ref[...] = carry_ref[...] + oh.sum(axis=0, keepdims=True)


def _slots(route, start):
    return pl.pallas_call(
        _slot_kernel,
        grid=(N_TOK // TR,),
        in_specs=[pl.BlockSpec((TR, LANES), lambda i: (i, 0)), pl.BlockSpec((1, LANES), lambda i: (0, 0))],
        out_specs=pl.BlockSpec((TR, LANES), lambda i: (i, 0)),
        out_shape=jax.ShapeDtypeStruct((N_TOK, LANES), F32),
        scratch_shapes=[pltpu.VMEM((1, LANES), F32)],
        compiler_params=_cparams(("arbitrary",)),
        name="expert_slots",
    )(route, start)


def _row_copy_waits(n, wait_rows):
    p = TB
    while p >= 1:
        @pl.when((n & p) != 0)
        def _(p=p):
            wait_rows(p)
        p //= 2


def _experts_kernel(cnt_ref, blk0_ref, tok_ref, dst_ref,
                    h_hbm, wg_ref, wu_ref, wd_ref,
                    out_hbm,
                    wg_s, wu_s, wd_s, xbuf, x2d, ybuf, gsem, ssem):
    e = pl.program_id(0)
    cnt = cnt_ref[e]
    nblk = (cnt + TB - 1) // TB
    slot0 = blk0_ref[e] * TB

    @pl.when(e == 0)
    def _():
        xbuf[...] = jnp.zeros_like(xbuf)

    def rows_in(b):
        return jnp.minimum(cnt - b * TB, TB)

    def slab_rows(i, n):
        start = i * SLAB
        return pl.ds(start if isinstance(i, int) else pl.multiple_of(start, SLAB), n * SLAB)

    def gather_desc(buf, r, t, n):
        return pltpu.make_async_copy(h_hbm.at[slab_rows(t, n)], xbuf.at[buf, slab_rows(r, n)], gsem.at[buf])

    def scatter_desc(r, d, n):
        return pltpu.make_async_copy(ybuf.at[slab_rows(r, n)], out_hbm.at[slab_rows(d, n)], ssem)

    def start_gather(b, buf):
        base = slot0 + b * TB

        def issue(r, carry):
            gather_desc(buf, r, tok_ref[base + r], 1).start()
            return carry

        lax.fori_loop(0, rows_in(b), issue, 0)

    def start_scatter(b):
        base = slot0 + b * TB

        def issue(r, carry):
            scatter_desc(r, dst_ref[base + r], 1).start()
            return carry

        lax.fori_loop(0, rows_in(b), issue, 0)

    @pl.when(nblk > 0)
    def _():
        start_gather(0, 0)
        wg_s[...] = wg_ref[...].astype(BF16)
        wu_s[...] = wu_ref[...].astype(BF16)
        wd_s[...] = wd_ref[...].astype(BF16)

        def block(b, carry):
            buf = b % 2

            @pl.when(b + 1 < nblk)
            def _():
                start_gather(b + 1, 1 - buf)

            _row_copy_waits(rows_in(b), lambda n: gather_desc(buf, 0, 0, n).wait())
            for c in range(SLAB):
                x2d[:, c * LANES:(c + 1) * LANES] = xbuf[buf, pl.ds(c, TB, stride=SLAB), :].astype(BF16)
            x = x2d[...]
            g = _bdot(x, wg_s[...])
            u = _bdot(x, wu_s[...])
            y = _bdot((_silu(g) * u).astype(BF16), wd_s[...])

            @pl.when(b > 0)
            def _():
                _row_copy_waits(rows_in(b - 1), lambda n: scatter_desc(0, 0, n).wait())

            for c in range(SLAB):
                ybuf[pl.ds(c, TB, stride=SLAB), :] = y[:, c * LANES:(c + 1) * LANES]
            start_scatter(b)
            return carry

        lax.fori_loop(0, nblk, block, 0)
        _row_copy_waits(rows_in(nblk - 1), lambda n: scatter_desc(0, 0, n).wait())


def _experts(h_slab, cnt, blk0, slot_tok, slot_dst, w_gate, w_up, w_down, layer):
    def w_idx(e, *_):
        return (layer, e, 0, 0)

    return pl.pallas_call(
        _experts_kernel,
        grid_spec=pltpu.PrefetchScalarGridSpec(
            num_scalar_prefetch=4,
            grid=(N_EXPERTS,),
            in_specs=[
                pl.BlockSpec(memory_space=pl.ANY),
                pl.BlockSpec((None, None, D, D_EXPERT), w_idx),
                pl.BlockSpec((None, None, D, D_EXPERT), w_idx),
                pl.BlockSpec((None, None, D_EXPERT, D), w_idx),
            ],
            out_specs=pl.BlockSpec(memory_space=pl.ANY),
            scratch_shapes=[pltpu.VMEM((D, D_EXPERT), BF16), pltpu.VMEM((D, D_EXPERT), BF16),
                            pltpu.VMEM((D_EXPERT, D), BF16),
                            pltpu.VMEM((2, TB * SLAB, LANES), F32),
                            pltpu.VMEM((TB, D), BF16),
                            pltpu.VMEM((TB * SLAB, LANES), F32),
                            pltpu.SemaphoreType.DMA((2,)),
                            pltpu.SemaphoreType.DMA(())],
        ),
        out_shape=jax.ShapeDtypeStruct((TOP_K * N_TOK * SLAB, LANES), F32),
        compiler_params=_cparams(("arbitrary",)),
        name="experts",
    )(cnt, blk0, slot_tok, slot_dst, h_slab, w_gate, w_up, w_down)


def _ffn_out_kernel(x_ref, r0_ref, r1_ref, route_ref, mod_ref, lg_ref, lb_ref, o_ref, *, tm):
    rt = route_ref[...]
    w0, w1 = rt[:, 2:3], rt[:, 3:4]
    f = jnp.concatenate(
        [r0_ref[pl.ds(c, tm, stride=SLAB), :] * w0 + r1_ref[pl.ds(c, tm, stride=SLAB), :] * w1
         for c in range(SLAB)], axis=1)
    z = ALPHA * x_ref[...] + mod_ref[5:6, :] * f
    o_ref[...] = _ln(z, lg_ref[...], lb_ref[...])


def _ffn_out(x, picked, route, mods_l, ln_g, ln_b):
    tm = TM_LN
    nb = N_TOK // tm
    row, vec, mod = _row_specs(tm)
    return pl.pallas_call(
        functools.partial(_ffn_out_kernel, tm=tm),
        grid=(nb,),
        in_specs=[row,
                  pl.BlockSpec((tm * SLAB, LANES), lambda i: (i, 0)),
                  pl.BlockSpec((tm * SLAB, LANES), lambda i: (i + nb, 0)),
                  pl.BlockSpec((tm, LANES), lambda i: (i, 0)),
                  mod, vec, vec],
        out_specs=row,
        out_shape=jax.ShapeDtypeStruct((N_TOK, D), F32),
        compiler_params=_cparams(("parallel",)),
        name="ffn_out",
    )(x, picked, picked, route, mods_l, ln_g.reshape(1, D), ln_b.reshape(1, D))


def _moe(x, mods_l, layer, w_rg, b_rg, w_re, b_re, w_gate, w_up, w_down, ln_g, ln_b):
    w_r = jnp.zeros((D, LANES), F32).at[:, :N_GROUPS].set(w_rg).at[:, N_GROUPS:N_GROUPS + N_EXPERTS].set(w_re)
    w_hi = w_r.astype(BF16)
    w_lo = (w_r - w_hi.astype(F32)).astype(BF16)
    b_r = jnp.zeros((1, LANES), F32).at[0, :N_GROUPS].set(b_rg).at[0, N_GROUPS:N_GROUPS + N_EXPERTS].set(b_re)
    h_slab, route, counts = _router(x, mods_l, jnp.concatenate([w_hi, w_lo], axis=1), b_r)
    cnt = counts[0, :N_EXPERTS].astype(I32)
    nblk = (cnt + TB - 1) // TB
    blk0 = jnp.cumsum(nblk) - nblk
    start = jnp.zeros((1, LANES), F32).at[0, :N_EXPERTS].set((blk0 * TB).astype(F32))
    slot = _slots(route, start)[:, :TOP_K].astype(I32)
    dst = jnp.arange(TOP_K, dtype=I32)[None, :] * N_TOK + jnp.arange(N_TOK, dtype=I32)[:, None]
    slot_dst = jnp.zeros((N_SLOTS,), I32).at[slot.reshape(-1)].set(dst.reshape(-1))
    slot_tok = slot_dst % N_TOK
    picked = _experts(h_slab, cnt, blk0.astype(I32), slot_tok, slot_dst, w_gate, w_up, w_down, layer)
    return _ffn_out(x, picked, route, mods_l, ln_g, ln_b)


def kernel(x_prompt, x_sample, c, cache_k_l1, cache_v_l1, cache_k_l3, cache_v_l3, c_ctx, w_mod, b_mod, ln_mix_g, ln_mix_b, ln_ffn_g, ln_ffn_b, conv_w_pw1, conv_b_pw1, conv_w_dw, conv_b_dw, conv_ln_g, conv_ln_b, conv_w_pw2, conv_b_pw2, attn_w_q, attn_w_k, attn_w_v, attn_w_o, attn_lambda_q1, attn_lambda_k1, attn_lambda_q2, attn_lambda_k2, attn_subln_g, router_w_group, router_b_group, router_w_expert, router_b_expert, moe_w_gate, moe_w_up, moe_w_down):
    caches = ((cache_k_l1, cache_v_l1), (cache_k_l3, cache_v_l3))
    x = jnp.concatenate([x_prompt.reshape(N_P, D), x_sample.reshape(N_S, D)], axis=0)
    cond = jnp.zeros((N_MOD_ROWS, D), F32).at[0].set(c_ctx).at[1:1 + B_S].set(c)
    mods = _modulation(cond, w_mod, b_mod).reshape(DEPTH, N_MOD_ROWS, 6, D)
    rope = _rope_tables()
    new_kv = []
    for i in range(DEPTH):
        j = i // 2
        mods_l = mods[i]
        if i % 2 == 0:
            u = _pw1_glu(x, mods_l, conv_w_pw1[j].astype(BF16), conv_b_pw1[j])
            u = _dwconv(u, conv_w_dw[j], conv_b_dw[j])
            x = _conv_out(u, x, mods_l, conv_w_pw2[j].astype(BF16), conv_b_pw2[j], conv_ln_g[j], conv_ln_b[j],
                          ln_mix_g[i], ln_mix_b[i])
        else:
            lam_init = 0.8 - 0.6 * math.exp(-0.3 * i)
            w_qkv = jnp.stack([attn_w_q[j], attn_w_k[j], attn_w_v[j]]).astype(BF16)
            qkv = _qkv_proj(x, mods_l, w_qkv, rope)
            new_kv.append(qkv[1, :N_P].reshape(B_P, SEQ_P, N_HEADS, HEAD_W))
            new_kv.append(qkv[2, :N_P].reshape(B_P, SEQ_P, N_HEADS, HEAD_W))
            lam_params = jnp.stack([attn_lambda_q1[j], attn_lambda_k1[j], attn_lambda_q2[j], attn_lambda_k2[j]])
            o_ctx, o_lat = _attention(qkv, lam_params, attn_subln_g[j], caches[j][0], caches[j][1], lam_init)
            x = _attn_out(o_ctx, o_lat, x, mods_l, attn_w_o[j].astype(BF16), ln_mix_g[i], ln_mix_b[i])
        x = _moe(x, mods_l, i, router_w_group[i], router_b_group[i], router_w_expert[i], router_b_expert[i],
                 moe_w_gate, moe_w_up, moe_w_down, ln_ffn_g[i], ln_ffn_b[i])
    return (x[:N_P].reshape(B_P, SEQ_P, D), x[N_P:].reshape(B_S, SEQ_S, D),
            new_kv[0], new_kv[1], new_kv[2], new_kv[3])
```

```python
import functools
import math

import jax
import jax.numpy as jnp
from jax import lax
from jax.experimental import pallas as pl
from jax.experimental.pallas import tpu as pltpu

F32 = jnp.float32
BF16 = jnp.bfloat16
I32 = jnp.int32

D = 2048
DEPTH = 4
B_P, SEQ_P = 16, 256
B_S, SEQ_S = 8, 1024
PAST = 256
N_P = B_P * SEQ_P
N_S = B_S * SEQ_S
N_TOK = N_P + N_S
GRID_W = 64
HEAD_DIM = 64
N_HEADS = D // (2 * HEAD_DIM)
HEAD_W = 2 * HEAD_DIM
ROPE_BASE = 10000.0
CONV_W = 31
CONV_HALO = 16
N_GROUPS = 8
EPG = 8
N_EXPERTS = N_GROUPS * EPG
TOP_K = 2
D_EXPERT = D // 4
ALPHA = (2.0 * DEPTH) ** 0.25
LN_EPS = 1e-5
N_MOD_ROWS = 16

LANES = 128
SUBLANES = 8
SLAB = D // LANES
VMEM_LIMIT = 56 * 1024 * 1024

TM = 512
TN = 512
TN_PW1 = 1024
TM_OUT = 512
TM_LN = 256
TC_CONV = 512
TQ = 256
HEADS_CTX = 16
HEADS_LAT = 4
TB = 128
N_SLOTS = N_TOK * TOP_K + N_EXPERTS * TB
N_BLK = N_SLOTS // TB
TR = 512
GATHER_CHUNK = 1024


def _cparams(sem):
    return pltpu.CompilerParams(dimension_semantics=sem, vmem_limit_bytes=VMEM_LIMIT)


def _mod_row(i, tm):
    t0 = i * tm
    return jnp.where(t0 < N_P, 0, (t0 - N_P) // SEQ_S + 1)


def _ln(z, g, b):
    mu = jnp.mean(z, axis=-1, keepdims=True)
    zc = z - mu
    var = jnp.mean(zc * zc, axis=-1, keepdims=True)
    return zc * lax.rsqrt(var + LN_EPS) * g + b


def _silu(x):
    return x * jax.nn.sigmoid(x)


def _bdot(a, b):
    return jnp.dot(a, b, preferred_element_type=F32)


def _mod_kernel(cond_ref, w_ref, b_ref, o_ref):
    s = _silu(cond_ref[...]).astype(BF16)
    o_ref[...] = _bdot(s, w_ref[...].astype(BF16)) + b_ref[...]


def _modulation(cond, w_mod, b_mod):
    tn = 1024
    n_out = 6 * D
    return pl.pallas_call(
        _mod_kernel,
        grid=(DEPTH, n_out // tn),
        in_specs=[
            pl.BlockSpec((N_MOD_ROWS, D), lambda l, j: (0, 0)),
            pl.BlockSpec((None, D, tn), lambda l, j: (l, 0, j)),
            pl.BlockSpec((None, 1, tn), lambda l, j: (l, 0, j)),
        ],
        out_specs=pl.BlockSpec((None, N_MOD_ROWS, tn), lambda l, j: (l, 0, j)),
        out_shape=jax.ShapeDtypeStruct((DEPTH, N_MOD_ROWS, n_out), F32),
        compiler_params=_cparams(("parallel", "parallel")),
        name="modulation",
    )(cond, w_mod, b_mod.reshape(DEPTH, 1, n_out))


def _pw1_kernel(x_ref, mod_ref, wa_ref, wg_ref, ba_ref, bg_ref, o_ref, h_ref):
    @pl.when(pl.program_id(1) == 0)
    def _():
        h_ref[...] = (x_ref[...] * (1.0 + mod_ref[1:2, :]) + mod_ref[0:1, :]).astype(BF16)

    h = h_ref[...]
    for c in range(TN_PW1 // TN):
        cols = slice(c * TN, (c + 1) * TN)
        a = _bdot(h, wa_ref[:, cols]) + ba_ref[:, cols]
        g = _bdot(h, wg_ref[:, cols]) + bg_ref[:, cols]
        o_ref[:, cols] = a * jax.nn.sigmoid(g)


def _pw1_glu(x, mods_l, w_bf, b):
    nj = D // TN_PW1
    b2 = b.reshape(1, 2 * D)
    return pl.pallas_call(
        _pw1_kernel,
        grid=(N_TOK // TM, nj),
        in_specs=[
            pl.BlockSpec((TM, D), lambda i, j: (i, 0)),
            pl.BlockSpec((None, 6, D), lambda i, j: (_mod_row(i, TM), 0, 0)),
            pl.BlockSpec((D, TN_PW1), lambda i, j: (0, j)),
            pl.BlockSpec((D, TN_PW1), lambda i, j: (0, j + nj)),
            pl.BlockSpec((1, TN_PW1), lambda i, j: (0, j)),
            pl.BlockSpec((1, TN_PW1), lambda i, j: (0, j + nj)),
        ],
        out_specs=pl.BlockSpec((TM, TN_PW1), lambda i, j: (i, j)),
        out_shape=jax.ShapeDtypeStruct((N_TOK, D), F32),
        scratch_shapes=[pltpu.VMEM((TM, D), BF16)],
        compiler_params=_cparams(("parallel", "arbitrary")),
        name="pw1_glu",
    )(x, mods_l, w_bf, w_bf, b2, b2)


def _dwconv_kernel(main_ref, top_ref, bot_ref, w_ref, b_ref, o_ref, win_ref, *, tm):
    i = pl.program_id(0)
    t0 = i * tm
    seq = jnp.where(t0 < N_P, SEQ_P, SEQ_S)
    pos = jnp.where(t0 < N_P, t0 % SEQ_P, (t0 - N_P) % SEQ_S)
    has_top = pos > 0
    has_bot = pos + tm < seq
    win_ref[0, 0:CONV_HALO, :] = jnp.where(has_top, top_ref[...], 0.0)
    win_ref[0, CONV_HALO:CONV_HALO + tm, :] = main_ref[...]
    win_ref[0, CONV_HALO + tm:, :] = jnp.where(has_bot, bot_ref[...], 0.0)
    span = tm + 2 * CONV_HALO - SUBLANES
    for s in range(1, SUBLANES):
        win_ref[s, 0:span, :] = win_ref[0, s:s + span, :]
    rs = 32
    off = CONV_HALO - CONV_W // 2
    for r0 in range(0, tm, rs):
        acc = jnp.zeros((rs, o_ref.shape[1]), F32)
        for k in range(CONV_W):
            a, s = divmod(off + k, SUBLANES)
            acc = acc + win_ref[s, r0 + a * SUBLANES:r0 + a * SUBLANES + rs, :] * w_ref[k:k + 1, :]
        o_ref[r0:r0 + rs, :] = acc + b_ref[...]


def _dwconv(u, w_dw, b_dw):
    tm, tc = TM_LN, TC_CONV
    hb = tm // CONV_HALO
    n_hblk = N_TOK // CONV_HALO
    w_pad = jnp.zeros((32, D), F32).at[:CONV_W].set(w_dw)
    return pl.pallas_call(
        functools.partial(_dwconv_kernel, tm=tm),
        grid=(N_TOK // tm, D // tc),
        in_specs=[
            pl.BlockSpec((tm, tc), lambda i, c: (i, c)),
            pl.BlockSpec((CONV_HALO, tc), lambda i, c: (jnp.maximum(i * hb - 1, 0), c)),
            pl.BlockSpec((CONV_HALO, tc), lambda i, c: (jnp.minimum((i + 1) * hb, n_hblk - 1), c)),
            pl.BlockSpec((32, tc), lambda i, c: (0, c)),
            pl.BlockSpec((1, tc), lambda i, c: (0, c)),
        ],
        out_specs=pl.BlockSpec((tm, tc), lambda i, c: (i, c)),
        out_shape=jax.ShapeDtypeStruct((N_TOK, D), F32),
        scratch_shapes=[pltpu.VMEM((SUBLANES, tm + 2 * CONV_HALO, tc), F32)],
        compiler_params=_cparams(("parallel", "parallel")),
        name="dwconv",
    )(u, u, u, w_pad, b_dw.reshape(1, D))


def _deepnorm(x, gate, m, g, b):
    return _ln(ALPHA * x + gate * m, g, b)


def _row_halves(tm):
    return [slice(r, r + TM_LN) for r in range(0, tm, TM_LN)]


def _conv_out_kernel(u_ref, x_ref, mod_ref, w_ref, b_ref, cg_ref, cb_ref, lg_ref, lb_ref, o_ref, *, tm):
    for rows in _row_halves(tm):
        u = _silu(_ln(u_ref[rows, :], cg_ref[...], cb_ref[...]))
        m = _bdot(u.astype(BF16), w_ref[...]) + b_ref[...]
        o_ref[rows, :] = _deepnorm(x_ref[rows, :], mod_ref[2:3, :], m, lg_ref[...], lb_ref[...])


def _attn_out_kernel(op_ref, os_ref, x_ref, mod_ref, w_ref, lg_ref, lb_ref, o_ref, *, tm):
    is_ctx = pl.program_id(0) * tm < N_P
    for rows in _row_halves(tm):
        o = jnp.where(is_ctx, op_ref[rows, :], os_ref[rows, :])
        m = _bdot(o.astype(BF16), w_ref[...])
        o_ref[rows, :] = _deepnorm(x_ref[rows, :], mod_ref[2:3, :], m, lg_ref[...], lb_ref[...])


def _row_specs(tm):
    row = pl.BlockSpec((tm, D), lambda i: (i, 0))
    vec = pl.BlockSpec((1, D), lambda i: (0, 0))
    mod = pl.BlockSpec((None, 6, D), lambda i: (_mod_row(i, tm), 0, 0))
    return row, vec, mod


def _conv_out(u, x, mods_l, w_bf, b, conv_ln_g, conv_ln_b, ln_g, ln_b):
    tm = TM_OUT
    row, vec, mod = _row_specs(tm)
    return pl.pallas_call(
        functools.partial(_conv_out_kernel, tm=tm),
        grid=(N_TOK // tm,),
        in_specs=[row, row, mod, pl.BlockSpec((D, D), lambda i: (0, 0)), vec, vec, vec, vec, vec],
        out_specs=row,
        out_shape=jax.ShapeDtypeStruct((N_TOK, D), F32),
        compiler_params=_cparams(("parallel",)),
        name="conv_out",
    )(u, x, mods_l, w_bf, b.reshape(1, D), conv_ln_g.reshape(1, D), conv_ln_b.reshape(1, D),
      ln_g.reshape(1, D), ln_b.reshape(1, D))


def _attn_out(o_ctx, o_lat, x, mods_l, w_bf, ln_g, ln_b):
    tm = TM_OUT
    row, vec, mod = _row_specs(tm)
    n_ctx = N_P // tm
    return pl.pallas_call(
        functools.partial(_attn_out_kernel, tm=tm),
        grid=(N_TOK // tm,),
        in_specs=[pl.BlockSpec((tm, D), lambda i: (jnp.minimum(i, n_ctx - 1), 0)),
                  pl.BlockSpec((tm, D), lambda i: (jnp.maximum(i - n_ctx, 0), 0)),
                  row, mod, pl.BlockSpec((D, D), lambda i: (0, 0)), vec, vec],
        out_specs=row,
        out_shape=jax.ShapeDtypeStruct((N_TOK, D), F32),
        compiler_params=_cparams(("parallel",)),
        name="attn_out",
    )(o_ctx, o_lat, x, mods_l, w_bf, ln_g.reshape(1, D), ln_b.reshape(1, D))


def _qkv_kernel(x_ref, mod_ref, w_ref, cos_ref, se_ref, so_ref, o_ref, h_ref):
    i, s = pl.program_id(0), pl.program_id(1)

    @pl.when(s == 0)
    def _():
        h_ref[...] = (x_ref[...] * (1.0 + mod_ref[1:2, :]) + mod_ref[0:1, :]).astype(BF16)

    rotate = (i * TM >= N_P) & (s < 2)
    chunks = [slice(c * TN, (c + 1) * TN) for c in range(D // TN)]

    @pl.when(rotate)
    def _():
        for cols in chunks:
            y = _bdot(h_ref[...], w_ref[:, cols])
            nxt = pltpu.roll(y, TN - 1, 1)
            prv = pltpu.roll(y, 1, 1)
            o_ref[:, cols] = y * cos_ref[...] + nxt * se_ref[...] + prv * so_ref[...]

    @pl.when(jnp.logical_not(rotate))
    def _():
        for cols in chunks:
            o_ref[:, cols] = _bdot(h_ref[...], w_ref[:, cols])


def _rope_tables():
    rows = SEQ_S // GRID_W
    row = jnp.repeat(jnp.arange(rows, dtype=F32), GRID_W)
    col = jnp.tile(jnp.arange(GRID_W, dtype=F32), rows)
    n_freq = HEAD_DIM // 4
    inv = ROPE_BASE ** (-jnp.arange(n_freq, dtype=F32) / n_freq)
    ang = jnp.concatenate([row[:, None] * inv, col[:, None] * inv], -1)
    cos = jnp.repeat(jnp.cos(ang), 2, axis=-1)
    sin = jnp.repeat(jnp.sin(ang), 2, axis=-1)
    even = (jnp.arange(HEAD_DIM) % 2 == 0)[None, :]
    sin_even = jnp.where(even, -sin, 0.0)
    sin_odd = jnp.where(even, 0.0, sin)
    reps = TN // HEAD_DIM
    return tuple(jnp.tile(t, (1, reps)) for t in (cos, sin_even, sin_odd))


def _qkv_proj(x, mods_l, w_qkv_bf, rope):
    tab = pl.BlockSpec(
        (TM, TN), lambda i, s: (jnp.where(i * TM >= N_P, ((i * TM - N_P) % SEQ_S) // TM, 0), 0))
    return pl.pallas_call(
        _qkv_kernel,
        grid=(N_TOK // TM, 3),
        in_specs=[
            pl.BlockSpec((TM, D), lambda i, s: (i, 0)),
            pl.BlockSpec((None, 6, D), lambda i, s: (_mod_row(i, TM), 0, 0)),
            pl.BlockSpec((None, D, D), lambda i, s: (s, 0, 0)),
            tab, tab, tab,
        ],
        out_specs=pl.BlockSpec((None, TM, D), lambda i, s: (s, i, 0)),
        out_shape=jax.ShapeDtypeStruct((3, N_TOK, D), F32),
        scratch_shapes=[pltpu.VMEM((TM, D), BF16)],
        compiler_params=_cparams(("parallel", "arbitrary")),
        name="qkv_proj",
    )(x, mods_l, w_qkv_bf, *rope)


def _softmax_parts(q_half, ks):
    dn = (((1,), (1,)), ((), ()))
    ss = [lax.dot_general(q_half, k, dn, preferred_element_type=F32) for k in ks]
    m = ss[0].max(axis=-1, keepdims=True)
    for s in ss[1:]:
        m = jnp.maximum(m, s.max(axis=-1, keepdims=True))
    ps = [jnp.exp(s - m) for s in ss]
    l = ps[0].sum(axis=-1, keepdims=True)
    for p in ps[1:]:
        l = l + p.sum(axis=-1, keepdims=True)
    return ps, 1.0 / l


def _attn_kernel(*refs, lam_init, has_cache, heads):
    if has_cache:
        lam_ref, g_ref, q_ref, k_ref, v_ref, kc_ref, vc_ref, o_ref = refs
    else:
        lam_ref, g_ref, q_ref, k_ref, v_ref, o_ref = refs
    lp = lam_ref[...]
    lam = (jnp.exp(jnp.sum(lp[0:1] * lp[1:2], axis=-1, keepdims=True))
           - jnp.exp(jnp.sum(lp[2:3] * lp[3:4], axis=-1, keepdims=True)) + lam_init)
    lane = lax.broadcasted_iota(I32, (q_ref.shape[0], HEAD_W), 1)
    for hd in range(heads):
        cols = slice(hd * HEAD_W, (hd + 1) * HEAD_W)
        q = q_ref[:, cols] * (HEAD_DIM ** -0.5)
        q1 = jnp.where(lane < HEAD_DIM, q, 0.0).astype(BF16)
        q2 = jnp.where(lane >= HEAD_DIM, q, 0.0).astype(BF16)
        ks = [k_ref[:, cols].astype(BF16)]
        vs = [v_ref[:, cols].astype(BF16)]
        if has_cache:
            ks.append(kc_ref[:, cols].astype(BF16))
            vs.append(vc_ref[:, cols].astype(BF16))
        p1, r1 = _softmax_parts(q1, ks)
        p2, r2 = _softmax_parts(q2, ks)
        r2 = lam * r2
        o = None
        for a1, a2, v in zip(p1, p2, vs):
            a = (a1 * r1 - a2 * r2).astype(BF16)
            t = _bdot(a, v)
            o = t if o is None else o + t
        ms = jnp.mean(o * o, axis=-1, keepdims=True)
        o_ref[:, cols] = o * lax.rsqrt(ms + LN_EPS) * g_ref[...] * (1.0 - lam_init)


def _attention(qkv, lam_params, subln_g, cache_k, cache_v, lam_init):
    g2 = subln_g.reshape(1, HEAD_W)
    small = [pl.BlockSpec((4, HEAD_DIM), lambda *_: (0, 0)), pl.BlockSpec((1, HEAD_W), lambda *_: (0, 0))]
    wc = HEADS_CTX * HEAD_W
    o_ctx = pl.pallas_call(
        functools.partial(_attn_kernel, lam_init=lam_init, has_cache=False, heads=HEADS_CTX),
        grid=(B_P, N_HEADS // HEADS_CTX),
        in_specs=small + [
            pl.BlockSpec((None, SEQ_P, wc), lambda b, h: (0, b, h)),
            pl.BlockSpec((None, SEQ_P, wc), lambda b, h: (1, b, h)),
            pl.BlockSpec((None, SEQ_P, wc), lambda b, h: (2, b, h)),
        ],
        out_specs=pl.BlockSpec((SEQ_P, wc), lambda b, h: (b, h)),
        out_shape=jax.ShapeDtypeStruct((N_P, D), F32),
        compiler_params=_cparams(("parallel", "parallel")),
        name="attn_ctx",
    )(lam_params, g2, qkv, qkv, qkv)
    nq = SEQ_S // TQ
    q0 = N_P // TQ
    s0 = N_P // SEQ_S
    wl = HEADS_LAT * HEAD_W
    o_lat = pl.pallas_call(
        functools.partial(_attn_kernel, lam_init=lam_init, has_cache=True, heads=HEADS_LAT),
        grid=(B_S, N_HEADS // HEADS_LAT, nq),
        in_specs=small + [
            pl.BlockSpec((None, TQ, wl), lambda b, h, t: (0, q0 + b * nq + t, h)),
            pl.BlockSpec((None, SEQ_S, wl), lambda b, h, t: (1, s0 + b, h)),
            pl.BlockSpec((None, SEQ_S, wl), lambda b, h, t: (2, s0 + b, h)),
            pl.BlockSpec((PAST, wl), lambda b, h, t: (b, h)),
            pl.BlockSpec((PAST, wl), lambda b, h, t: (b, h)),
        ],
        out_specs=pl.BlockSpec((TQ, wl), lambda b, h, t: (b * nq + t, h)),
        out_shape=jax.ShapeDtypeStruct((N_S, D), F32),
        compiler_params=_cparams(("parallel", "parallel", "arbitrary")),
        name="attn_lat",
    )(lam_params, g2, qkv, qkv, qkv, cache_k.reshape(B_S * PAST, D), cache_v.reshape(B_S * PAST, D))
    return o_ctx, o_lat


def _router_kernel(x_ref, mod_ref, w_ref, b_ref, h_ref, r_ref, cnt_ref, carry_ref, *, tm):
    @pl.when(pl.program_id(0) == 0)
    def _():
        carry_ref[...] = jnp.zeros_like(carry_ref)

    h = x_ref[...] * (1.0 + mod_ref[4:5, :]) + mod_ref[3:4, :]
    for c in range(SLAB):
        h_ref[pl.ds(c, tm, stride=SLAB), :] = h[:, c * LANES:(c + 1) * LANES]
    hh = h.astype(BF16)
    hl = (h - hh.astype(F32)).astype(BF16)
    w = w_ref[...]
    t = _bdot(hh, w)
    logit = t[:, :LANES] + t[:, LANES:] + _bdot(hl, w[:, :LANES]) + b_ref[...]
    lane = lax.broadcasted_iota(I32, logit.shape, 1).astype(F32)
    neg = -jnp.inf
    big = 4.0 * LANES
    lg = jnp.where(lane < N_GROUPS, logit, neg)
    gmax = lg.max(axis=-1, keepdims=True)
    grp = jnp.where(lg == gmax, lane, big).min(axis=-1, keepdims=True)
    pg = 1.0 / jnp.exp(lg - gmax).sum(axis=-1, keepdims=True)
    lo = N_GROUPS + grp * EPG
    le = jnp.where((lane >= lo) & (lane < lo + EPG), logit, neg)
    v1 = le.max(axis=-1, keepdims=True)
    i1 = jnp.where(le == v1, lane, big).min(axis=-1, keepdims=True)
    le2 = jnp.where(lane == i1, neg, le)
    v2 = le2.max(axis=-1, keepdims=True)
    i2 = jnp.where(le2 == v2, lane, big).min(axis=-1, keepdims=True)
    e = jnp.exp(v2 - v1)
    den = 1.0 / (1.0 + e)
    w1 = pg * den
    w2 = pg * (e * den)
    e1 = i1 - N_GROUPS
    e2 = i2 - N_GROUPS
    r_ref[...] = jnp.where(lane == 0, e1, jnp.where(lane == 1, e2, jnp.where(lane == 2, w1, jnp.where(lane == 3, w2, 0.0))))
    picked = jnp.where(lane == e1, 1.0, jnp.where(lane == e2, 1.0, 0.0))
    carry_ref[...] = carry_ref[...] + picked.sum(axis=0, keepdims=True)
    cnt_ref[...] = jnp.broadcast_to(carry_ref[...], cnt_ref.shape)


def _router(x, mods_l, w_hl, b_r):
    tm = TM_LN
    return pl.pallas_call(
        functools.partial(_router_kernel, tm=tm),
        grid=(N_TOK // tm,),
        in_specs=[
            pl.BlockSpec((tm, D), lambda i: (i, 0)),
            pl.BlockSpec((None, 6, D), lambda i: (_mod_row(i, tm), 0, 0)),
            pl.BlockSpec((D, 2 * LANES), lambda i: (0, 0)),
            pl.BlockSpec((1, LANES), lambda i: (0, 0)),
        ],
        out_specs=[pl.BlockSpec((tm * SLAB, LANES), lambda i: (i, 0)),
                   pl.BlockSpec((tm, LANES), lambda i: (i, 0)),
                   pl.BlockSpec((SUBLANES, LANES), lambda i: (0, 0))],
        out_shape=[jax.ShapeDtypeStruct((N_TOK * SLAB, LANES), F32),
                   jax.ShapeDtypeStruct((N_TOK, LANES), F32),
                   jax.ShapeDtypeStruct((SUBLANES, LANES), F32)],
        scratch_shapes=[pltpu.VMEM((1, LANES), F32)],
        compiler_params=_cparams(("arbitrary",)),
        name="router",
    )(x, mods_l, w_hl, b_r)


def _slot_kernel(r_ref, start_ref, slot_ref, carry_ref):
    @pl.when(pl.program_id(0) == 0)
    def _():
        carry_ref[...] = jnp.zeros_like(carry_ref)

    r = r_ref[...]
    lane = lax.broadcasted_iota(I32, r.shape, 1)
    e1 = r[:, 0:1].astype(I32)
    e2 = r[:, 1:2].astype(I32)
    oh1 = (lane == e1).astype(F32)
    oh2 = (lane == e2).astype(F32)
    oh = oh1 + oh2
    rr = lax.broadcasted_iota(I32, (TR, TR), 0)
    cc = lax.broadcasted_iota(I32, (TR, TR), 1)
    tri = (cc < rr).astype(BF16)
    before = _bdot(tri, oh.astype(BF16)) + carry_ref[...] + start_ref[...]
    k1 = (oh1 * before).sum(axis=-1, keepdims=True)
    k2 = (oh2 * before).sum(axis=-1, keepdims=True)
    slot_ref[...] = jnp.where(lane == 0, k1, jnp.where(lane == 1, k2, 0.0))
    carry_ref[...] = carry_ref[...] + oh.sum(axis=0, keepdims=True)


def _slots(route, start):
    return pl.pallas_call(
        _slot_kernel,
        grid=(N_TOK // TR,),
        in_specs=[pl.BlockSpec((TR, LANES), lambda i: (i, 0)), pl.BlockSpec((1, LANES), lambda i: (0, 0))],
        out_specs=pl.BlockSpec((TR, LANES), lambda i: (i, 0)),
        out_shape=jax.ShapeDtypeStruct((N_TOK, LANES), F32),
        scratch_shapes=[pltpu.VMEM((1, LANES), F32)],
        compiler_params=_cparams(("arbitrary",)),
        name="expert_slots",
    )(route, start)


def _row_copy_waits(n, wait_rows):
    p = TB
    while p >= 1:
        @pl.when((n & p) != 0)
        def _(p=p):
            wait_rows(p)
        p //= 2


def _experts_kernel(cnt_ref, blk0_ref, tok_ref, dst_ref,
                    h_hbm, wg_ref, wu_ref, wd_ref,
                    out_hbm,
                    wg_s, wu_s, wd_s, xbuf, x2d, ybuf, gsem, ssem):
    e = pl.program_id(0)
    cnt = cnt_ref[e]
    nblk = (cnt + TB - 1) // TB
    slot0 = blk0_ref[e] * TB

    @pl.when(e == 0)
    def _():
        xbuf[...] = jnp.zeros_like(xbuf)

    def rows_in(b):
        return jnp.minimum(cnt - b * TB, TB)

    def slab_rows(i, n):
        start = i * SLAB
        return pl.ds(start if isinstance(i, int) else pl.multiple_of(start, SLAB), n * SLAB)

    def gather_desc(buf, r, t, n):
        return pltpu.make_async_copy(h_hbm.at[slab_rows(t, n)], xbuf.at[buf, slab_rows(r, n)], gsem.at[buf])

    def scatter_desc(r, d, n):
        return pltpu.make_async_copy(ybuf.at[slab_rows(r, n)], out_hbm.at[slab_rows(d, n)], ssem)

    def start_gather(b, buf):
        base = slot0 + b * TB

        def issue(r, carry):
            gather_desc(buf, r, tok_ref[base + r], 1).start(priority=1)
            return carry

        lax.fori_loop(0, rows_in(b), issue, 0)

    def start_scatter(b):
        base = slot0 + b * TB

        def issue(r, carry):
            scatter_desc(r, dst_ref[base + r], 1).start()
            return carry

        lax.fori_loop(0, rows_in(b), issue, 0)

    @pl.when(nblk > 0)
    def _():
        start_gather(0, 0)
        wg_s[...] = wg_ref[...].astype(BF16)
        wu_s[...] = wu_ref[...].astype(BF16)
        wd_s[...] = wd_ref[...].astype(BF16)

        def block(b, carry):
            buf = b % 2

            @pl.when(b + 1 < nblk)
            def _():
                start_gather(b + 1, 1 - buf)

            _row_copy_waits(rows_in(b), lambda n: gather_desc(buf, 0, 0, n).wait())
            for c in range(SLAB):
                x2d[:, c * LANES:(c + 1) * LANES] = xbuf[buf, pl.ds(c, TB, stride=SLAB), :].astype(BF16)
            x = x2d[...]
            g = _bdot(x, wg_s[...])
            u = _bdot(x, wu_s[...])
            y = _bdot((_silu(g) * u).astype(BF16), wd_s[...])

            @pl.when(b > 0)
            def _():
                _row_copy_waits(rows_in(b - 1), lambda n: scatter_desc(0, 0, n).wait())

            for c in range(SLAB):
                ybuf[pl.ds(c, TB, stride=SLAB), :] = y[:, c * LANES:(c + 1) * LANES]
            start_scatter(b)
            return carry

        lax.fori_loop(0, nblk, block, 0)
        _row_copy_waits(rows_in(nblk - 1), lambda n: scatter_desc(0, 0, n).wait())


def _experts(h_slab, cnt, blk0, slot_tok, slot_dst, w_gate, w_up, w_down, layer):
    def w_idx(e, *_):
        return (layer, e, 0, 0)

    return pl.pallas_call(
        _experts_kernel,
        grid_spec=pltpu.PrefetchScalarGridSpec(
            num_scalar_prefetch=4,
            grid=(N_EXPERTS,),
            in_specs=[
                pl.BlockSpec(memory_space=pl.ANY),
                pl.BlockSpec((None, None, D, D_EXPERT), w_idx),
                pl.BlockSpec((None, None, D, D_EXPERT), w_idx),
                pl.BlockSpec((None, None, D_EXPERT, D), w_idx),
            ],
            out_specs=pl.BlockSpec(memory_space=pl.ANY),
            scratch_shapes=[pltpu.VMEM((D, D_EXPERT), BF16), pltpu.VMEM((D, D_EXPERT), BF16),
                            pltpu.VMEM((D_EXPERT, D), BF16),
                            pltpu.VMEM((2, TB * SLAB, LANES), F32),
                            pltpu.VMEM((TB, D), BF16),
                            pltpu.VMEM((TB * SLAB, LANES), F32),
                            pltpu.SemaphoreType.DMA((2,)),
                            pltpu.SemaphoreType.DMA(())],
        ),
        out_shape=jax.ShapeDtypeStruct((TOP_K * N_TOK * SLAB, LANES), F32),
        compiler_params=_cparams(("arbitrary",)),
        name="experts",
    )(cnt, blk0, slot_tok, slot_dst, h_slab, w_gate, w_up, w_down)


def _ffn_out_kernel(x_ref, r0_ref, r1_ref, route_ref, mod_ref, lg_ref, lb_ref, o_ref, *, tm):
    rt = route_ref[...]
    w0, w1 = rt[:, 2:3], rt[:, 3:4]
    f = jnp.concatenate(
        [r0_ref[pl.ds(c, tm, stride=SLAB), :] * w0 + r1_ref[pl.ds(c, tm, stride=SLAB), :] * w1
         for c in range(SLAB)], axis=1)
    z = ALPHA * x_ref[...] + mod_ref[5:6, :] * f
    o_ref[...] = _ln(z, lg_ref[...], lb_ref[...])


def _ffn_out(x, picked, route, mods_l, ln_g, ln_b):
    tm = TM_LN
    nb = N_TOK // tm
    row, vec, mod = _row_specs(tm)
    return pl.pallas_call(
        functools.partial(_ffn_out_kernel, tm=tm),
        grid=(nb,),
        in_specs=[row,
                  pl.BlockSpec((tm * SLAB, LANES), lambda i: (i, 0)),
                  pl.BlockSpec((tm * SLAB, LANES), lambda i: (i + nb, 0)),
                  pl.BlockSpec((tm, LANES), lambda i: (i, 0)),
                  mod, vec, vec],
        out_specs=row,
        out_shape=jax.ShapeDtypeStruct((N_TOK, D), F32),
        compiler_params=_cparams(("parallel",)),
        name="ffn_out",
    )(x, picked, picked, route, mods_l, ln_g.reshape(1, D), ln_b.reshape(1, D))


def _moe(x, mods_l, layer, w_rg, b_rg, w_re, b_re, w_gate, w_up, w_down, ln_g, ln_b):
    w_r = jnp.zeros((D, LANES), F32).at[:, :N_GROUPS].set(w_rg).at[:, N_GROUPS:N_GROUPS + N_EXPERTS].set(w_re)
    w_hi = w_r.astype(BF16)
    w_lo = (w_r - w_hi.astype(F32)).astype(BF16)
    b_r = jnp.zeros((1, LANES), F32).at[0, :N_GROUPS].set(b_rg).at[0, N_GROUPS:N_GROUPS + N_EXPERTS].set(b_re)
    h_slab, route, counts = _router(x, mods_l, jnp.concatenate([w_hi, w_lo], axis=1), b_r)
    cnt = counts[0, :N_EXPERTS].astype(I32)
    nblk = (cnt + TB - 1) // TB
    blk0 = jnp.cumsum(nblk) - nblk
    start = jnp.zeros((1, LANES), F32).at[0, :N_EXPERTS].set((blk0 * TB).astype(F32))
    slot = _slots(route, start)[:, :TOP_K].astype(I32)
    dst = jnp.arange(TOP_K, dtype=I32)[None, :] * N_TOK + jnp.arange(N_TOK, dtype=I32)[:, None]
    slot_dst = jnp.zeros((N_SLOTS,), I32).at[slot.reshape(-1)].set(dst.reshape(-1))
    slot_tok = slot_dst % N_TOK
    picked = _experts(h_slab, cnt, blk0.astype(I32), slot_tok, slot_dst, w_gate, w_up, w_down, layer)
    return _ffn_out(x, picked, route, mods_l, ln_g, ln_b)


def kernel(x_prompt, x_sample, c, cache_k_l1, cache_v_l1, cache_k_l3, cache_v_l3, c_ctx, w_mod, b_mod, ln_mix_g, ln_mix_b, ln_ffn_g, ln_ffn_b, conv_w_pw1, conv_b_pw1, conv_w_dw, conv_b_dw, conv_ln_g, conv_ln_b, conv_w_pw2, conv_b_pw2, attn_w_q, attn_w_k, attn_w_v, attn_w_o, attn_lambda_q1, attn_lambda_k1, attn_lambda_q2, attn_lambda_k2, attn_subln_g, router_w_group, router_b_group, router_w_expert, router_b_expert, moe_w_gate, moe_w_up, moe_w_down):
    caches = ((cache_k_l1, cache_v_l1), (cache_k_l3, cache_v_l3))
    x = jnp.concatenate([x_prompt.reshape(N_P, D), x_sample.reshape(N_S, D)], axis=0)
    cond = jnp.zeros((N_MOD_ROWS, D), F32).at[0].set(c_ctx).at[1:1 + B_S].set(c)
    mods = _modulation(cond, w_mod, b_mod).reshape(DEPTH, N_MOD_ROWS, 6, D)
    rope = _rope_tables()
    new_kv = []
    for i in range(DEPTH):
        j = i // 2
        mods_l = mods[i]
        if i % 2 == 0:
            u = _pw1_glu(x, mods_l, conv_w_pw1[j].astype(BF16), conv_b_pw1[j])
            u = _dwconv(u, conv_w_dw[j], conv_b_dw[j])
            x = _conv_out(u, x, mods_l, conv_w_pw2[j].astype(BF16), conv_b_pw2[j], conv_ln_g[j], conv_ln_b[j],
                          ln_mix_g[i], ln_mix_b[i])
        else:
            lam_init = 0.8 - 0.6 * math.exp(-0.3 * i)
            w_qkv = jnp.stack([attn_w_q[j], attn_w_k[j], attn_w_v[j]]).astype(BF16)
            qkv = _qkv_proj(x, mods_l, w_qkv, rope)
            new_kv.append(qkv[1, :N_P].reshape(B_P, SEQ_P, N_HEADS, HEAD_W))
            new_kv.append(qkv[2, :N_P].reshape(B_P, SEQ_P, N_HEADS, HEAD_W))
            lam_params = jnp.stack([attn_lambda_q1[j], attn_lambda_k1[j], attn_lambda_q2[j], attn_lambda_k2[j]])
            o_ctx, o_lat = _attention(qkv, lam_params, attn_subln_g[j], caches[j][0], caches[j][1], lam_init)
            x = _attn_out(o_ctx, o_lat, x, mods_l, attn_w_o[j].astype(BF16), ln_mix_g[i], ln_mix_b[i])
        x = _moe(x, mods_l, i, router_w_group[i], router_b_group[i], router_w_expert[i], router_b_expert[i],
                 moe_w_gate, moe_w_up, moe_w_down, ln_ffn_g[i], ln_ffn_b[i])
    return (x[:N_P].reshape(B_P, SEQ_P, D), x[N_P:].reshape(B_S, SEQ_S, D),
            new_kv[0], new_kv[1], new_kv[2], new_kv[3])
```

```python
import functools
import math

import jax
import jax.numpy as jnp
from jax import lax
from jax.experimental import pallas as pl
from jax.experimental.pallas import tpu as pltpu

F32 = jnp.float32
BF16 = jnp.bfloat16
I32 = jnp.int32

D = 2048
DEPTH = 4
B_P, SEQ_P = 16, 256
B_S, SEQ_S = 8, 1024
PAST = 256
N_P = B_P * SEQ_P
N_S = B_S * SEQ_S
N_TOK = N_P + N_S
GRID_W = 64
HEAD_DIM = 64
N_HEADS = D // (2 * HEAD_DIM)
HEAD_W = 2 * HEAD_DIM
ROPE_BASE = 10000.0
CONV_W = 31
CONV_HALO = 16
N_GROUPS = 8
EPG = 8
N_EXPERTS = N_GROUPS * EPG
TOP_K = 2
D_EXPERT = D // 4
ALPHA = (2.0 * DEPTH) ** 0.25
LN_EPS = 1e-5
N_MOD_ROWS = 16

LANES = 128
SUBLANES = 8
SLAB = D // LANES
VMEM_LIMIT = 56 * 1024 * 1024

TM = 512
TN = 512
TN_PW1 = 1024
TM_OUT = 512
TM_LN = 256
TC_CONV = 512
TQ = 256
HEADS_CTX = 16
HEADS_LAT = 4
TB = 128
N_SLOTS = N_TOK * TOP_K + N_EXPERTS * TB
N_BLK = N_SLOTS // TB
TR = 512
GATHER_CHUNK = 1024


def _cparams(sem):
    return pltpu.CompilerParams(dimension_semantics=sem, vmem_limit_bytes=VMEM_LIMIT)


def _mod_row(i, tm):
    t0 = i * tm
    return jnp.where(t0 < N_P, 0, (t0 - N_P) // SEQ_S + 1)


def _ln(z, g, b):
    mu = jnp.mean(z, axis=-1, keepdims=True)
    zc = z - mu
    var = jnp.mean(zc * zc, axis=-1, keepdims=True)
    return zc * lax.rsqrt(var + LN_EPS) * g + b


def _silu(x):
    return x * jax.nn.sigmoid(x)


def _bdot(a, b):
    return jnp.dot(a, b, preferred_element_type=F32)


def _mod_kernel(cond_ref, w_ref, b_ref, o_ref):
    s = _silu(cond_ref[...]).astype(BF16)
    o_ref[...] = _bdot(s, w_ref[...].astype(BF16)) + b_ref[...]


def _modulation(cond, w_mod, b_mod):
    tn = 1024
    n_out = 6 * D
    return pl.pallas_call(
        _mod_kernel,
        grid=(DEPTH, n_out // tn),
        in_specs=[
            pl.BlockSpec((N_MOD_ROWS, D), lambda l, j: (0, 0)),
            pl.BlockSpec((None, D, tn), lambda l, j: (l, 0, j)),
            pl.BlockSpec((None, 1, tn), lambda l, j: (l, 0, j)),
        ],
        out_specs=pl.BlockSpec((None, N_MOD_ROWS, tn), lambda l, j: (l, 0, j)),
        out_shape=jax.ShapeDtypeStruct((DEPTH, N_MOD_ROWS, n_out), F32),
        compiler_params=_cparams(("parallel", "parallel")),
        name="modulation",
    )(cond, w_mod, b_mod.reshape(DEPTH, 1, n_out))


def _pw1_kernel(x_ref, mod_ref, wa_ref, wg_ref, ba_ref, bg_ref, o_ref, h_ref):
    @pl.when(pl.program_id(1) == 0)
    def _():
        h_ref[...] = (x_ref[...] * (1.0 + mod_ref[1:2, :]) + mod_ref[0:1, :]).astype(BF16)

    h = h_ref[...]
    for c in range(TN_PW1 // TN):
        cols = slice(c * TN, (c + 1) * TN)
        a = _bdot(h, wa_ref[:, cols]) + ba_ref[:, cols]
        g = _bdot(h, wg_ref[:, cols]) + bg_ref[:, cols]
        o_ref[:, cols] = a * jax.nn.sigmoid(g)


def _pw1_glu(x, mods_l, w_bf, b):
    nj = D // TN_PW1
    b2 = b.reshape(1, 2 * D)
    return pl.pallas_call(
        _pw1_kernel,
        grid=(N_TOK // TM, nj),
        in_specs=[
            pl.BlockSpec((TM, D), lambda i, j: (i, 0)),
            pl.BlockSpec((None, 6, D), lambda i, j: (_mod_row(i, TM), 0, 0)),
            pl.BlockSpec((D, TN_PW1), lambda i, j: (0, j)),
            pl.BlockSpec((D, TN_PW1), lambda i, j: (0, j + nj)),
            pl.BlockSpec((1, TN_PW1), lambda i, j: (0, j)),
            pl.BlockSpec((1, TN_PW1), lambda i, j: (0, j + nj)),
        ],
        out_specs=pl.BlockSpec((TM, TN_PW1), lambda i, j: (i, j)),
        out_shape=jax.ShapeDtypeStruct((N_TOK, D), F32),
        scratch_shapes=[pltpu.VMEM((TM, D), BF16)],
        compiler_params=_cparams(("parallel", "arbitrary")),
        name="pw1_glu",
    )(x, mods_l, w_bf, w_bf, b2, b2)


def _dwconv_kernel(main_ref, top_ref, bot_ref, w_ref, b_ref, o_ref, win_ref, *, tm):
    i = pl.program_id(0)
    t0 = i * tm
    seq = jnp.where(t0 < N_P, SEQ_P, SEQ_S)
    pos = jnp.where(t0 < N_P, t0 % SEQ_P, (t0 - N_P) % SEQ_S)
    has_top = pos > 0
    has_bot = pos + tm < seq
    win_ref[0, 0:CONV_HALO, :] = jnp.where(has_top, top_ref[...], 0.0)
    win_ref[0, CONV_HALO:CONV_HALO + tm, :] = main_ref[...]
    win_ref[0, CONV_HALO + tm:, :] = jnp.where(has_bot, bot_ref[...], 0.0)
    span = tm + 2 * CONV_HALO - SUBLANES
    for s in range(1, SUBLANES):
        win_ref[s, 0:span, :] = win_ref[0, s:s + span, :]
    rs = 32
    off = CONV_HALO - CONV_W // 2
    for r0 in range(0, tm, rs):
        acc = jnp.zeros((rs, o_ref.shape[1]), F32)
        for k in range(CONV_W):
            a, s = divmod(off + k, SUBLANES)
            acc = acc + win_ref[s, r0 + a * SUBLANES:r0 + a * SUBLANES + rs, :] * w_ref[k:k + 1, :]
        o_ref[r0:r0 + rs, :] = acc + b_ref[...]


def _dwconv(u, w_dw, b_dw):
    tm, tc = TM_LN, TC_CONV
    hb = tm // CONV_HALO
    n_hblk = N_TOK // CONV_HALO
    w_pad = jnp.zeros((32, D), F32).at[:CONV_W].set(w_dw)
    return pl.pallas_call(
        functools.partial(_dwconv_kernel, tm=tm),
        grid=(N_TOK // tm, D // tc),
        in_specs=[
            pl.BlockSpec((tm, tc), lambda i, c: (i, c)),
            pl.BlockSpec((CONV_HALO, tc), lambda i, c: (jnp.maximum(i * hb - 1, 0), c)),
            pl.BlockSpec((CONV_HALO, tc), lambda i, c: (jnp.minimum((i + 1) * hb, n_hblk - 1), c)),
            pl.BlockSpec((32, tc), lambda i, c: (0, c)),
            pl.BlockSpec((1, tc), lambda i, c: (0, c)),
        ],
        out_specs=pl.BlockSpec((tm, tc), lambda i, c: (i, c)),
        out_shape=jax.ShapeDtypeStruct((N_TOK, D), F32),
        scratch_shapes=[pltpu.VMEM((SUBLANES, tm + 2 * CONV_HALO, tc), F32)],
        compiler_params=_cparams(("parallel", "parallel")),
        name="dwconv",
    )(u, u, u, w_pad, b_dw.reshape(1, D))


def _deepnorm(x, gate, m, g, b):
    return _ln(ALPHA * x + gate * m, g, b)


def _row_halves(tm):
    return [slice(r, r + TM_LN) for r in range(0, tm, TM_LN)]


def _conv_out_kernel(u_ref, x_ref, mod_ref, w_ref, b_ref, cg_ref, cb_ref, lg_ref, lb_ref, o_ref, *, tm):
    for rows in _row_halves(tm):
        u = _silu(_ln(u_ref[rows, :], cg_ref[...], cb_ref[...]))
        m = _bdot(u.astype(BF16), w_ref[...]) + b_ref[...]
        o_ref[rows, :] = _deepnorm(x_ref[rows, :], mod_ref[2:3, :], m, lg_ref[...], lb_ref[...])


def _attn_out_kernel(op_ref, os_ref, x_ref, mod_ref, w_ref, lg_ref, lb_ref, o_ref, *, tm):
    is_ctx = pl.program_id(0) * tm < N_P
    for rows in _row_halves(tm):
        o = jnp.where(is_ctx, op_ref[rows, :], os_ref[rows, :])
        m = _bdot(o.astype(BF16), w_ref[...])
        o_ref[rows, :] = _deepnorm(x_ref[rows, :], mod_ref[2:3, :], m, lg_ref[...], lb_ref[...])


def _row_specs(tm):
    row = pl.BlockSpec((tm, D), lambda i: (i, 0))
    vec = pl.BlockSpec((1, D), lambda i: (0, 0))
    mod = pl.BlockSpec((None, 6, D), lambda i: (_mod_row(i, tm), 0, 0))
    return row, vec, mod


def _conv_out(u, x, mods_l, w_bf, b, conv_ln_g, conv_ln_b, ln_g, ln_b):
    tm = TM_OUT
    row, vec, mod = _row_specs(tm)
    return pl.pallas_call(
        functools.partial(_conv_out_kernel, tm=tm),
        grid=(N_TOK // tm,),
        in_specs=[row, row, mod, pl.BlockSpec((D, D), lambda i: (0, 0)), vec, vec, vec, vec, vec],
        out_specs=row,
        out_shape=jax.ShapeDtypeStruct((N_TOK, D), F32),
        compiler_params=_cparams(("parallel",)),
        name="conv_out",
    )(u, x, mods_l, w_bf, b.reshape(1, D), conv_ln_g.reshape(1, D), conv_ln_b.reshape(1, D),
      ln_g.reshape(1, D), ln_b.reshape(1, D))


def _attn_out(o_ctx, o_lat, x, mods_l, w_bf, ln_g, ln_b):
    tm = TM_OUT
    row, vec, mod = _row_specs(tm)
    n_ctx = N_P // tm
    return pl.pallas_call(
        functools.partial(_attn_out_kernel, tm=tm),
        grid=(N_TOK // tm,),
        in_specs=[pl.BlockSpec((tm, D), lambda i: (jnp.minimum(i, n_ctx - 1), 0)),
                  pl.BlockSpec((tm, D), lambda i: (jnp.maximum(i - n_ctx, 0), 0)),
                  row, mod, pl.BlockSpec((D, D), lambda i: (0, 0)), vec, vec],
        out_specs=row,
        out_shape=jax.ShapeDtypeStruct((N_TOK, D), F32),
        compiler_params=_cparams(("parallel",)),
        name="attn_out",
    )(o_ctx, o_lat, x, mods_l, w_bf, ln_g.reshape(1, D), ln_b.reshape(1, D))


def _qkv_kernel(x_ref, mod_ref, w_ref, cos_ref, se_ref, so_ref, o_ref, h_ref):
    i, s = pl.program_id(0), pl.program_id(1)

    @pl.when(s == 0)
    def _():
        h_ref[...] = (x_ref[...] * (1.0 + mod_ref[1:2, :]) + mod_ref[0:1, :]).astype(BF16)

    rotate = (i * TM >= N_P) & (s < 2)
    chunks = [slice(c * TN, (c + 1) * TN) for c in range(D // TN)]

    @pl.when(rotate)
    def _():
        for cols in chunks:
            y = _bdot(h_ref[...], w_ref[:, cols])
            nxt = pltpu.roll(y, TN - 1, 1)
            prv = pltpu.roll(y, 1, 1)
            o_ref[:, cols] = y * cos_ref[...] + nxt * se_ref[...] + prv * so_ref[...]

    @pl.when(jnp.logical_not(rotate))
    def _():
        for cols in chunks:
            o_ref[:, cols] = _bdot(h_ref[...], w_ref[:, cols])


def _rope_tables():
    rows = SEQ_S // GRID_W
    row = jnp.repeat(jnp.arange(rows, dtype=F32), GRID_W)
    col = jnp.tile(jnp.arange(GRID_W, dtype=F32), rows)
    n_freq = HEAD_DIM // 4
    inv = ROPE_BASE ** (-jnp.arange(n_freq, dtype=F32) / n_freq)
    ang = jnp.concatenate([row[:, None] * inv, col[:, None] * inv], -1)
    cos = jnp.repeat(jnp.cos(ang), 2, axis=-1)
    sin = jnp.repeat(jnp.sin(ang), 2, axis=-1)
    even = (jnp.arange(HEAD_DIM) % 2 == 0)[None, :]
    sin_even = jnp.where(even, -sin, 0.0)
    sin_odd = jnp.where(even, 0.0, sin)
    reps = TN // HEAD_DIM
    return tuple(jnp.tile(t, (1, reps)) for t in (cos, sin_even, sin_odd))


def _qkv_proj(x, mods_l, w_qkv_bf, rope):
    tab = pl.BlockSpec(
        (TM, TN), lambda i, s: (jnp.where(i * TM >= N_P, ((i * TM - N_P) % SEQ_S) // TM, 0), 0))
    return pl.pallas_call(
        _qkv_kernel,
        grid=(N_TOK // TM, 3),
        in_specs=[
            pl.BlockSpec((TM, D), lambda i, s: (i, 0)),
            pl.BlockSpec((None, 6, D), lambda i, s: (_mod_row(i, TM), 0, 0)),
            pl.BlockSpec((None, D, D), lambda i, s: (s, 0, 0)),
            tab, tab, tab,
        ],
        out_specs=pl.BlockSpec((None, TM, D), lambda i, s: (s, i, 0)),
        out_shape=jax.ShapeDtypeStruct((3, N_TOK, D), F32),
        scratch_shapes=[pltpu.VMEM((TM, D), BF16)],
        compiler_params=_cparams(("parallel", "arbitrary")),
        name="qkv_proj",
    )(x, mods_l, w_qkv_bf, *rope)


def _softmax_parts(q_half, ks):
    dn = (((1,), (1,)), ((), ()))
    ss = [lax.dot_general(q_half, k, dn, preferred_element_type=F32) for k in ks]
    m = ss[0].max(axis=-1, keepdims=True)
    for s in ss[1:]:
        m = jnp.maximum(m, s.max(axis=-1, keepdims=True))
    ps = [jnp.exp(s - m) for s in ss]
    l = ps[0].sum(axis=-1, keepdims=True)
    for p in ps[1:]:
        l = l + p.sum(axis=-1, keepdims=True)
    return ps, 1.0 / l


def _attn_kernel(*refs, lam_init, has_cache, heads):
    if has_cache:
        lam_ref, g_ref, q_ref, k_ref, v_ref, kc_ref, vc_ref, o_ref = refs
    else:
        lam_ref, g_ref, q_ref, k_ref, v_ref, o_ref = refs
    lp = lam_ref[...]
    lam = (jnp.exp(jnp.sum(lp[0:1] * lp[1:2], axis=-1, keepdims=True))
           - jnp.exp(jnp.sum(lp[2:3] * lp[3:4], axis=-1, keepdims=True)) + lam_init)
    lane = lax.broadcasted_iota(I32, (q_ref.shape[0], HEAD_W), 1)
    for hd in range(heads):
        cols = slice(hd * HEAD_W, (hd + 1) * HEAD_W)
        q = q_ref[:, cols] * (HEAD_DIM ** -0.5)
        q1 = jnp.where(lane < HEAD_DIM, q, 0.0).astype(BF16)
        q2 = jnp.where(lane >= HEAD_DIM, q, 0.0).astype(BF16)
        ks = [k_ref[:, cols].astype(BF16)]
        vs = [v_ref[:, cols].astype(BF16)]
        if has_cache:
            ks.append(kc_ref[:, cols].astype(BF16))
            vs.append(vc_ref[:, cols].astype(BF16))
        p1, r1 = _softmax_parts(q1, ks)
        p2, r2 = _softmax_parts(q2, ks)
        r2 = lam * r2
        o = None
        for a1, a2, v in zip(p1, p2, vs):
            a = (a1 * r1 - a2 * r2).astype(BF16)
            t = _bdot(a, v)
            o = t if o is None else o + t
        ms = jnp.mean(o * o, axis=-1, keepdims=True)
        o_ref[:, cols] = o * lax.rsqrt(ms + LN_EPS) * g_ref[...] * (1.0 - lam_init)


def _attention(qkv, lam_params, subln_g, cache_k, cache_v, lam_init):
    g2 = subln_g.reshape(1, HEAD_W)
    small = [pl.BlockSpec((4, HEAD_DIM), lambda *_: (0, 0)), pl.BlockSpec((1, HEAD_W), lambda *_: (0, 0))]
    wc = HEADS_CTX * HEAD_W
    o_ctx = pl.pallas_call(
        functools.partial(_attn_kernel, lam_init=lam_init, has_cache=False, heads=HEADS_CTX),
        grid=(B_P, N_HEADS // HEADS_CTX),
        in_specs=small + [
            pl.BlockSpec((None, SEQ_P, wc), lambda b, h: (0, b, h)),
            pl.BlockSpec((None, SEQ_P, wc), lambda b, h: (1, b, h)),
            pl.BlockSpec((None, SEQ_P, wc), lambda b, h: (2, b, h)),
        ],
        out_specs=pl.BlockSpec((SEQ_P, wc), lambda b, h: (b, h)),
        out_shape=jax.ShapeDtypeStruct((N_P, D), F32),
        compiler_params=_cparams(("parallel", "parallel")),
        name="attn_ctx",
    )(lam_params, g2, qkv, qkv, qkv)
    nq = SEQ_S // TQ
    q0 = N_P // TQ
    s0 = N_P // SEQ_S
    wl = HEADS_LAT * HEAD_W
    o_lat = pl.pallas_call(
        functools.partial(_attn_kernel, lam_init=lam_init, has_cache=True, heads=HEADS_LAT),
        grid=(B_S, N_HEADS // HEADS_LAT, nq),
        in_specs=small + [
            pl.BlockSpec((None, TQ, wl), lambda b, h, t: (0, q0 + b * nq + t, h)),
            pl.BlockSpec((None, SEQ_S, wl), lambda b, h, t: (1, s0 + b, h)),
            pl.BlockSpec((None, SEQ_S, wl), lambda b, h, t: (2, s0 + b, h)),
            pl.BlockSpec((PAST, wl), lambda b, h, t: (b, h)),
            pl.BlockSpec((PAST, wl), lambda b, h, t: (b, h)),
        ],
        out_specs=pl.BlockSpec((TQ, wl), lambda b, h, t: (b * nq + t, h)),
        out_shape=jax.ShapeDtypeStruct((N_S, D), F32),
        compiler_params=_cparams(("parallel", "parallel", "arbitrary")),
        name="attn_lat",
    )(lam_params, g2, qkv, qkv, qkv, cache_k.reshape(B_S * PAST, D), cache_v.reshape(B_S * PAST, D))
    return o_ctx, o_lat


def _router_kernel(x_ref, mod_ref, w_ref, b_ref, h_ref, r_ref, cnt_ref, carry_ref, *, tm):
    @pl.when(pl.program_id(0) == 0)
    def _():
        carry_ref[...] = jnp.zeros_like(carry_ref)

    h = x_ref[...] * (1.0 + mod_ref[4:5, :]) + mod_ref[3:4, :]
    for c in range(SLAB):
        h_ref[pl.ds(c, tm, stride=SLAB), :] = h[:, c * LANES:(c + 1) * LANES]
    hh = h.astype(BF16)
    hl = (h - hh.astype(F32)).astype(BF16)
    w = w_ref[...]
    t = _bdot(hh, w)
    logit = t[:, :LANES] + t[:, LANES:] + _bdot(hl, w[:, :LANES]) + b_ref[...]
    lane = lax.broadcasted_iota(I32, logit.shape, 1).astype(F32)
    neg = -jnp.inf
    big = 4.0 * LANES
    lg = jnp.where(lane < N_GROUPS, logit, neg)
    gmax = lg.max(axis=-1, keepdims=True)
    grp = jnp.where(lg == gmax, lane, big).min(axis=-1, keepdims=True)
    pg = 1.0 / jnp.exp(lg - gmax).sum(axis=-1, keepdims=True)
    lo = N_GROUPS + grp * EPG
    le = jnp.where((lane >= lo) & (lane < lo + EPG), logit, neg)
    v1 = le.max(axis=-1, keepdims=True)
    i1 = jnp.where(le == v1, lane, big).min(axis=-1, keepdims=True)
    le2 = jnp.where(lane == i1, neg, le)
    v2 = le2.max(axis=-1, keepdims=True)
    i2 = jnp.where(le2 == v2, lane, big).min(axis=-1, keepdims=True)
    e = jnp.exp(v2 - v1)
    den = 1.0 / (1.0 + e)
    w1 = pg * den
    w2 = pg * (e * den)
    e1 = i1 - N_GROUPS
    e2 = i2 - N_GROUPS
    r_ref[...] = jnp.where(lane == 0, e1, jnp.where(lane == 1, e2, jnp.where(lane == 2, w1, jnp.where(lane == 3, w2, 0.0))))
    picked = jnp.where(lane == e1, 1.0, jnp.where(lane == e2, 1.0, 0.0))
    carry_ref[...] = carry_ref[...] + picked.sum(axis=0, keepdims=True)
    cnt_ref[...] = jnp.broadcast_to(carry_ref[...], cnt_ref.shape)


def _router(x, mods_l, w_hl, b_r):
    tm = TM_LN
    return pl.pallas_call(
        functools.partial(_router_kernel, tm=tm),
        grid=(N_TOK // tm,),
        in_specs=[
            pl.BlockSpec((tm, D), lambda i: (i, 0)),
            pl.BlockSpec((None, 6, D), lambda i: (_mod_row(i, tm), 0, 0)),
            pl.BlockSpec((D, 2 * LANES), lambda i: (0, 0)),
            pl.BlockSpec((1, LANES), lambda i: (0, 0)),
        ],
        out_specs=[pl.BlockSpec((tm * SLAB, LANES), lambda i: (i, 0)),
                   pl.BlockSpec((tm, LANES), lambda i: (i, 0)),
                   pl.BlockSpec((SUBLANES, LANES), lambda i: (0, 0))],
        out_shape=[jax.ShapeDtypeStruct((N_TOK * SLAB, LANES), F32),
                   jax.ShapeDtypeStruct((N_TOK, LANES), F32),
                   jax.ShapeDtypeStruct((SUBLANES, LANES), F32)],
        scratch_shapes=[pltpu.VMEM((1, LANES), F32)],
        compiler_params=_cparams(("arbitrary",)),
        name="router",
    )(x, mods_l, w_hl, b_r)


def _slot_kernel(r_ref, start_ref, slot_ref, carry_ref):
    @pl.when(pl.program_id(0) == 0)
    def _():
        carry_ref[...] = jnp.zeros_like(carry_ref)

    r = r_ref[...]
    lane = lax.broadcasted_iota(I32, r.shape, 1)
    e1 = r[:, 0:1].astype(I32)
    e2 = r[:, 1:2].astype(I32)
    oh1 = (lane == e1).astype(F32)
    oh2 = (lane == e2).astype(F32)
    oh = oh1 + oh2
    rr = lax.broadcasted_iota(I32, (TR, TR), 0)
    cc = lax.broadcasted_iota(I32, (TR, TR), 1)
    tri = (cc < rr).astype(BF16)
    before = _bdot(tri, oh.astype(BF16)) + carry_ref[...] + start_ref[...]
    k1 = (oh1 * before).sum(axis=-1, keepdims=True)
    k2 = (oh2 * before).sum(axis=-1, keepdims=True)
    slot_ref[...] = jnp.where(lane == 0, k1, jnp.where(lane == 1, k2, 0.0))
    carry_ref[...] = carry_ref[...] + oh.sum(axis=0, keepdims=True)


def _slots(route, start):
    return pl.pallas_call(
        _slot_kernel,
        grid=(N_TOK // TR,),
        in_specs=[pl.BlockSpec((TR, LANES), lambda i: (i, 0)), pl.BlockSpec((1, LANES), lambda i: (0, 0))],
        out_specs=pl.BlockSpec((TR, LANES), lambda i: (i, 0)),
        out_shape=jax.ShapeDtypeStruct((N_TOK, LANES), F32),
        scratch_shapes=[pltpu.VMEM((1, LANES), F32)],
        compiler_params=_cparams(("arbitrary",)),
        name="expert_slots",
    )(route, start)


def _row_copy_waits(n, wait_rows):
    p = TB
    while p >= 1:
        @pl.when((n & p) != 0)
        def _(p=p):
            wait_rows(p)
        p //= 2


def _experts_kernel(cnt_ref, blk0_ref, tok_ref, dst_ref,
                    h_hbm, wg_hbm, wu_hbm, wd_hbm,
                    out_hbm,
                    wg_f, wu_f, wd_f, wg_s, wu_s, wd_s, xbuf, x2d, ybuf, wsem, gsem, ssem, *, layer):
    e = pl.program_id(0)
    cnt = cnt_ref[e]
    nblk = (cnt + TB - 1) // TB
    slot0 = blk0_ref[e] * TB

    def weight_copies(ex, wbuf):
        return [pltpu.make_async_copy(src.at[layer, ex], dst.at[wbuf], wsem.at[wbuf, k])
                for k, (src, dst) in enumerate(((wg_hbm, wg_f), (wu_hbm, wu_f), (wd_hbm, wd_f)))]

    @pl.when(e == 0)
    def _():
        for cp in weight_copies(0, 0):
            cp.start(priority=1)

    @pl.when(e + 1 < N_EXPERTS)
    def _():
        for cp in weight_copies(e + 1, (e + 1) % 2):
            cp.start(priority=1)

    @pl.when(e == 0)
    def _():
        xbuf[...] = jnp.zeros_like(xbuf)

    def rows_in(b):
        return jnp.minimum(cnt - b * TB, TB)

    def slab_rows(i, n):
        start = i * SLAB
        return pl.ds(start if isinstance(i, int) else pl.multiple_of(start, SLAB), n * SLAB)

    def gather_desc(buf, r, t, n):
        return pltpu.make_async_copy(h_hbm.at[slab_rows(t, n)], xbuf.at[buf, slab_rows(r, n)], gsem.at[buf])

    def scatter_desc(r, d, n):
        return pltpu.make_async_copy(ybuf.at[slab_rows(r, n)], out_hbm.at[slab_rows(d, n)], ssem)

    def start_gather(b, buf):
        base = slot0 + b * TB

        def issue(r, carry):
            gather_desc(buf, r, tok_ref[base + r], 1).start()
            return carry

        lax.fori_loop(0, rows_in(b), issue, 0)

    def start_scatter(b):
        base = slot0 + b * TB

        def issue(r, carry):
            scatter_desc(r, dst_ref[base + r], 1).start()
            return carry

        lax.fori_loop(0, rows_in(b), issue, 0)

    @pl.when(nblk > 0)
    def _():
        start_gather(0, 0)

    wbuf = e % 2
    for cp in weight_copies(e, wbuf):
        cp.wait()

    @pl.when(nblk > 0)
    def _():
        wg_s[...] = wg_f[wbuf].astype(BF16)
        wu_s[...] = wu_f[wbuf].astype(BF16)
        wd_s[...] = wd_f[wbuf].astype(BF16)

        def block(b, carry):
            buf = b % 2

            @pl.when(b + 1 < nblk)
            def _():
                start_gather(b + 1, 1 - buf)

            _row_copy_waits(rows_in(b), lambda n: gather_desc(buf, 0, 0, n).wait())
            for c in range(SLAB):
                x2d[:, c * LANES:(c + 1) * LANES] = xbuf[buf, pl.ds(c, TB, stride=SLAB), :].astype(BF16)
            x = x2d[...]
            g = _bdot(x, wg_s[...])
            u = _bdot(x, wu_s[...])
            y = _bdot((_silu(g) * u).astype(BF16), wd_s[...])

            @pl.when(b > 0)
            def _():
                _row_copy_waits(rows_in(b - 1), lambda n: scatter_desc(0, 0, n).wait())

            for c in range(SLAB):
                ybuf[pl.ds(c, TB, stride=SLAB), :] = y[:, c * LANES:(c + 1) * LANES]
            start_scatter(b)
            return carry

        lax.fori_loop(0, nblk, block, 0)
        _row_copy_waits(rows_in(nblk - 1), lambda n: scatter_desc(0, 0, n).wait())


def _experts(h_slab, cnt, blk0, slot_tok, slot_dst, w_gate, w_up, w_down, layer):
    hbm = pl.BlockSpec(memory_space=pl.ANY)
    return pl.pallas_call(
        functools.partial(_experts_kernel, layer=layer),
        grid_spec=pltpu.PrefetchScalarGridSpec(
            num_scalar_prefetch=4,
            grid=(N_EXPERTS,),
            in_specs=[hbm, hbm, hbm, hbm],
            out_specs=hbm,
            scratch_shapes=[pltpu.VMEM((2, D, D_EXPERT), F32), pltpu.VMEM((2, D, D_EXPERT), F32),
                            pltpu.VMEM((2, D_EXPERT, D), F32),
                            pltpu.VMEM((D, D_EXPERT), BF16), pltpu.VMEM((D, D_EXPERT), BF16),
                            pltpu.VMEM((D_EXPERT, D), BF16),
                            pltpu.VMEM((2, TB * SLAB, LANES), F32),
                            pltpu.VMEM((TB, D), BF16),
                            pltpu.VMEM((TB * SLAB, LANES), F32),
                            pltpu.SemaphoreType.DMA((2, 3)),
                            pltpu.SemaphoreType.DMA((2,)),
                            pltpu.SemaphoreType.DMA(())],
        ),
        out_shape=jax.ShapeDtypeStruct((TOP_K * N_TOK * SLAB, LANES), F32),
        compiler_params=_cparams(("arbitrary",)),
        name="experts",
    )(cnt, blk0, slot_tok, slot_dst, h_slab, w_gate, w_up, w_down)


def _ffn_out_kernel(x_ref, r0_ref, r1_ref, route_ref, mod_ref, lg_ref, lb_ref, o_ref, *, tm):
    rt = route_ref[...]
    w0, w1 = rt[:, 2:3], rt[:, 3:4]
    f = jnp.concatenate(
        [r0_ref[pl.ds(c, tm, stride=SLAB), :] * w0 + r1_ref[pl.ds(c, tm, stride=SLAB), :] * w1
         for c in range(SLAB)], axis=1)
    z = ALPHA * x_ref[...] + mod_ref[5:6, :] * f
    o_ref[...] = _ln(z, lg_ref[...], lb_ref[...])


def _ffn_out(x, picked, route, mods_l, ln_g, ln_b):
    tm = TM_LN
    nb = N_TOK // tm
    row, vec, mod = _row_specs(tm)
    return pl.pallas_call(
        functools.partial(_ffn_out_kernel, tm=tm),
        grid=(nb,),
        in_specs=[row,
                  pl.BlockSpec((tm * SLAB, LANES), lambda i: (i, 0)),
                  pl.BlockSpec((tm * SLAB, LANES), lambda i: (i + nb, 0)),
                  pl.BlockSpec((tm, LANES), lambda i: (i, 0)),
                  mod, vec, vec],
        out_specs=row,
        out_shape=jax.ShapeDtypeStruct((N_TOK, D), F32),
        compiler_params=_cparams(("parallel",)),
        name="ffn_out",
    )(x, picked, picked, route, mods_l, ln_g.reshape(1, D), ln_b.reshape(1, D))


def _moe(x, mods_l, layer, w_rg, b_rg, w_re, b_re, w_gate, w_up, w_down, ln_g, ln_b):
    w_r = jnp.zeros((D, LANES), F32).at[:, :N_GROUPS].set(w_rg).at[:, N_GROUPS:N_GROUPS + N_EXPERTS].set(w_re)
    w_hi = w_r.astype(BF16)
    w_lo = (w_r - w_hi.astype(F32)).astype(BF16)
    b_r = jnp.zeros((1, LANES), F32).at[0, :N_GROUPS].set(b_rg).at[0, N_GROUPS:N_GROUPS + N_EXPERTS].set(b_re)
    h_slab, route, counts = _router(x, mods_l, jnp.concatenate([w_hi, w_lo], axis=1), b_r)
    cnt = counts[0, :N_EXPERTS].astype(I32)
    nblk = (cnt + TB - 1) // TB
    blk0 = jnp.cumsum(nblk) - nblk
    start = jnp.zeros((1, LANES), F32).at[0, :N_EXPERTS].set((blk0 * TB).astype(F32))
    slot = _slots(route, start)[:, :TOP_K].astype(I32)
    dst = jnp.arange(TOP_K, dtype=I32)[None, :] * N_TOK + jnp.arange(N_TOK, dtype=I32)[:, None]
    slot_dst = jnp.zeros((N_SLOTS,), I32).at[slot.reshape(-1)].set(dst.reshape(-1))
    slot_tok = slot_dst % N_TOK
    picked = _experts(h_slab, cnt, blk0.astype(I32), slot_tok, slot_dst, w_gate, w_up, w_down, layer)
    return _ffn_out(x, picked, route, mods_l, ln_g, ln_b)


def kernel(x_prompt, x_sample, c, cache_k_l1, cache_v_l1, cache_k_l3, cache_v_l3, c_ctx, w_mod, b_mod, ln_mix_g, ln_mix_b, ln_ffn_g, ln_ffn_b, conv_w_pw1, conv_b_pw1, conv_w_dw, conv_b_dw, conv_ln_g, conv_ln_b, conv_w_pw2, conv_b_pw2, attn_w_q, attn_w_k, attn_w_v, attn_w_o, attn_lambda_q1, attn_lambda_k1, attn_lambda_q2, attn_lambda_k2, attn_subln_g, router_w_group, router_b_group, router_w_expert, router_b_expert, moe_w_gate, moe_w_up, moe_w_down):
    caches = ((cache_k_l1, cache_v_l1), (cache_k_l3, cache_v_l3))
    x = jnp.concatenate([x_prompt.reshape(N_P, D), x_sample.reshape(N_S, D)], axis=0)
    cond = jnp.zeros((N_MOD_ROWS, D), F32).at[0].set(c_ctx).at[1:1 + B_S].set(c)
    mods = _modulation(cond, w_mod, b_mod).reshape(DEPTH, N_MOD_ROWS, 6, D)
    rope = _rope_tables()
    new_kv = []
    for i in range(DEPTH):
        j = i // 2
        mods_l = mods[i]
        if i % 2 == 0:
            u = _pw1_glu(x, mods_l, conv_w_pw1[j].astype(BF16), conv_b_pw1[j])
            u = _dwconv(u, conv_w_dw[j], conv_b_dw[j])
            x = _conv_out(u, x, mods_l, conv_w_pw2[j].astype(BF16), conv_b_pw2[j], conv_ln_g[j], conv_ln_b[j],
                          ln_mix_g[i], ln_mix_b[i])
        else:
            lam_init = 0.8 - 0.6 * math.exp(-0.3 * i)
            w_qkv = jnp.stack([attn_w_q[j], attn_w_k[j], attn_w_v[j]]).astype(BF16)
            qkv = _qkv_proj(x, mods_l, w_qkv, rope)
            new_kv.append(qkv[1, :N_P].reshape(B_P, SEQ_P, N_HEADS, HEAD_W))
            new_kv.append(qkv[2, :N_P].reshape(B_P, SEQ_P, N_HEADS, HEAD_W))
            lam_params = jnp.stack([attn_lambda_q1[j], attn_lambda_k1[j], attn_lambda_q2[j], attn_lambda_k2[j]])
            o_ctx, o_lat = _attention(qkv, lam_params, attn_subln_g[j], caches[j][0], caches[j][1], lam_init)
            x = _attn_out(o_ctx, o_lat, x, mods_l, attn_w_o[j].astype(BF16), ln_mix_g[i], ln_mix_b[i])
        x = _moe(x, mods_l, i, router_w_group[i], router_b_group[i], router_w_expert[i], router_b_expert[i],
                 moe_w_gate, moe_w_up, moe_w_down, ln_ffn_g[i], ln_ffn_b[i])
    return (x[:N_P].reshape(B_P, SEQ_P, D), x[N_P:].reshape(B_S, SEQ_S, D),
            new_kv[0], new_kv[1], new_kv[2], new_kv[3])
```

```python
import functools
import math

import jax
import jax.numpy as jnp
from jax import lax
from jax.experimental import pallas as pl
from jax.experimental.pallas import tpu as pltpu

F32 = jnp.float32
BF16 = jnp.bfloat16
I32 = jnp.int32

D = 2048
DEPTH = 4
B_P, SEQ_P = 16, 256
B_S, SEQ_S = 8, 1024
PAST = 256
N_P = B_P * SEQ_P
N_S = B_S * SEQ_S
N_TOK = N_P + N_S
GRID_W = 64
HEAD_DIM = 64
N_HEADS = D // (2 * HEAD_DIM)
HEAD_W = 2 * HEAD_DIM
ROPE_BASE = 10000.0
CONV_W = 31
CONV_HALO = 16
N_GROUPS = 8
EPG = 8
N_EXPERTS = N_GROUPS * EPG
TOP_K = 2
D_EXPERT = D // 4
ALPHA = (2.0 * DEPTH) ** 0.25
LN_EPS = 1e-5
N_MOD_ROWS = 16

LANES = 128
SUBLANES = 8
SLAB = D // LANES
VMEM_LIMIT = 56 * 1024 * 1024

TM = 512
TN = 512
TN_PW1 = 1024
TM_OUT = 512
TM_LN = 256
TC_CONV = 512
TQ = 256
HEADS_CTX = 16
HEADS_LAT = 4
TB = 128
N_SLOTS = N_TOK * TOP_K + N_EXPERTS * TB
N_BLK = N_SLOTS // TB
TR = 512
GATHER_CHUNK = 1024


def _cparams(sem):
    return pltpu.CompilerParams(dimension_semantics=sem, vmem_limit_bytes=VMEM_LIMIT)


def _mod_row(i, tm):
    t0 = i * tm
    return jnp.where(t0 < N_P, 0, (t0 - N_P) // SEQ_S + 1)


def _ln(z, g, b):
    mu = jnp.mean(z, axis=-1, keepdims=True)
    zc = z - mu
    var = jnp.mean(zc * zc, axis=-1, keepdims=True)
    return zc * lax.rsqrt(var + LN_EPS) * g + b


def _silu(x):
    return x * jax.nn.sigmoid(x)


def _bdot(a, b):
    return jnp.dot(a, b, preferred_element_type=F32)


def _mod_kernel(cond_ref, w_ref, b_ref, o_ref):
    s = _silu(cond_ref[...]).astype(BF16)
    o_ref[...] = _bdot(s, w_ref[...].astype(BF16)) + b_ref[...]


def _modulation(cond, w_mod, b_mod):
    tn = 1024
    n_out = 6 * D
    return pl.pallas_call(
        _mod_kernel,
        grid=(DEPTH, n_out // tn),
        in_specs=[
            pl.BlockSpec((N_MOD_ROWS, D), lambda l, j: (0, 0)),
            pl.BlockSpec((None, D, tn), lambda l, j: (l, 0, j)),
            pl.BlockSpec((None, 1, tn), lambda l, j: (l, 0, j)),
        ],
        out_specs=pl.BlockSpec((None, N_MOD_ROWS, tn), lambda l, j: (l, 0, j)),
        out_shape=jax.ShapeDtypeStruct((DEPTH, N_MOD_ROWS, n_out), F32),
        compiler_params=_cparams(("parallel", "parallel")),
        name="modulation",
    )(cond, w_mod, b_mod.reshape(DEPTH, 1, n_out))


def _pw1_kernel(x_ref, mod_ref, wa_ref, wg_ref, ba_ref, bg_ref, o_ref, h_ref):
    @pl.when(pl.program_id(1) == 0)
    def _():
        h_ref[...] = (x_ref[...] * (1.0 + mod_ref[1:2, :]) + mod_ref[0:1, :]).astype(BF16)

    h = h_ref[...]
    for c in range(TN_PW1 // TN):
        cols = slice(c * TN, (c + 1) * TN)
        a = _bdot(h, wa_ref[:, cols]) + ba_ref[:, cols]
        g = _bdot(h, wg_ref[:, cols]) + bg_ref[:, cols]
        o_ref[:, cols] = a * jax.nn.sigmoid(g)


def _pw1_glu(x, mods_l, w_bf, b):
    nj = D // TN_PW1
    b2 = b.reshape(1, 2 * D)
    return pl.pallas_call(
        _pw1_kernel,
        grid=(N_TOK // TM, nj),
        in_specs=[
            pl.BlockSpec((TM, D), lambda i, j: (i, 0)),
            pl.BlockSpec((None, 6, D), lambda i, j: (_mod_row(i, TM), 0, 0)),
            pl.BlockSpec((D, TN_PW1), lambda i, j: (0, j)),
            pl.BlockSpec((D, TN_PW1), lambda i, j: (0, j + nj)),
            pl.BlockSpec((1, TN_PW1), lambda i, j: (0, j)),
            pl.BlockSpec((1, TN_PW1), lambda i, j: (0, j + nj)),
        ],
        out_specs=pl.BlockSpec((TM, TN_PW1), lambda i, j: (i, j)),
        out_shape=jax.ShapeDtypeStruct((N_TOK, D), F32),
        scratch_shapes=[pltpu.VMEM((TM, D), BF16)],
        compiler_params=_cparams(("parallel", "arbitrary")),
        name="pw1_glu",
    )(x, mods_l, w_bf, w_bf, b2, b2)


def _dwconv_kernel(main_ref, top_ref, bot_ref, w_ref, b_ref, o_ref, win_ref, *, tm):
    i = pl.program_id(0)
    t0 = i * tm
    seq = jnp.where(t0 < N_P, SEQ_P, SEQ_S)
    pos = jnp.where(t0 < N_P, t0 % SEQ_P, (t0 - N_P) % SEQ_S)
    has_top = pos > 0
    has_bot = pos + tm < seq
    win_ref[0, 0:CONV_HALO, :] = jnp.where(has_top, top_ref[...], 0.0)
    win_ref[0, CONV_HALO:CONV_HALO + tm, :] = main_ref[...]
    win_ref[0, CONV_HALO + tm:, :] = jnp.where(has_bot, bot_ref[...], 0.0)
    span = tm + 2 * CONV_HALO - SUBLANES
    for s in range(1, SUBLANES):
        win_ref[s, 0:span, :] = win_ref[0, s:s + span, :]
    rs = 32
    off = CONV_HALO - CONV_W // 2
    for r0 in range(0, tm, rs):
        acc = jnp.zeros((rs, o_ref.shape[1]), F32)
        for k in range(CONV_W):
            a, s = divmod(off + k, SUBLANES)
            acc = acc + win_ref[s, r0 + a * SUBLANES:r0 + a * SUBLANES + rs, :] * w_ref[k:k + 1, :]
        o_ref[r0:r0 + rs, :] = acc + b_ref[...]


def _dwconv(u, w_dw, b_dw):
    tm, tc = TM_LN, TC_CONV
    hb = tm // CONV_HALO
    n_hblk = N_TOK // CONV_HALO
    w_pad = jnp.zeros((32, D), F32).at[:CONV_W].set(w_dw)
    return pl.pallas_call(
        functools.partial(_dwconv_kernel, tm=tm),
        grid=(N_TOK // tm, D // tc),
        in_specs=[
            pl.BlockSpec((tm, tc), lambda i, c: (i, c)),
            pl.BlockSpec((CONV_HALO, tc), lambda i, c: (jnp.maximum(i * hb - 1, 0), c)),
            pl.BlockSpec((CONV_HALO, tc), lambda i, c: (jnp.minimum((i + 1) * hb, n_hblk - 1), c)),
            pl.BlockSpec((32, tc), lambda i, c: (0, c)),
            pl.BlockSpec((1, tc), lambda i, c: (0, c)),
        ],
        out_specs=pl.BlockSpec((tm, tc), lambda i, c: (i, c)),
        out_shape=jax.ShapeDtypeStruct((N_TOK, D), F32),
        scratch_shapes=[pltpu.VMEM((SUBLANES, tm + 2 * CONV_HALO, tc), F32)],
        compiler_params=_cparams(("parallel", "parallel")),
        name="dwconv",
    )(u, u, u, w_pad, b_dw.reshape(1, D))


def _deepnorm(x, gate, m, g, b):
    return _ln(ALPHA * x + gate * m, g, b)


def _row_halves(tm):
    return [slice(r, r + TM_LN) for r in range(0, tm, TM_LN)]


def _conv_out_kernel(u_ref, x_ref, mod_ref, w_ref, b_ref, cg_ref, cb_ref, lg_ref, lb_ref, o_ref, *, tm):
    for rows in _row_halves(tm):
        u = _silu(_ln(u_ref[rows, :], cg_ref[...], cb_ref[...]))
        m = _bdot(u.astype(BF16), w_ref[...]) + b_ref[...]
        o_ref[rows, :] = _deepnorm(x_ref[rows, :], mod_ref[2:3, :], m, lg_ref[...], lb_ref[...])


def _attn_out_kernel(op_ref, os_ref, x_ref, mod_ref, w_ref, lg_ref, lb_ref, o_ref, *, tm):
    is_ctx = pl.program_id(0) * tm < N_P
    for rows in _row_halves(tm):
        o = jnp.where(is_ctx, op_ref[rows, :], os_ref[rows, :])
        m = _bdot(o.astype(BF16), w_ref[...])
        o_ref[rows, :] = _deepnorm(x_ref[rows, :], mod_ref[2:3, :], m, lg_ref[...], lb_ref[...])


def _row_specs(tm):
    row = pl.BlockSpec((tm, D), lambda i: (i, 0))
    vec = pl.BlockSpec((1, D), lambda i: (0, 0))
    mod = pl.BlockSpec((None, 6, D), lambda i: (_mod_row(i, tm), 0, 0))
    return row, vec, mod


def _conv_out(u, x, mods_l, w_bf, b, conv_ln_g, conv_ln_b, ln_g, ln_b):
    tm = TM_OUT
    row, vec, mod = _row_specs(tm)
    return pl.pallas_call(
        functools.partial(_conv_out_kernel, tm=tm),
        grid=(N_TOK // tm,),
        in_specs=[row, row, mod, pl.BlockSpec((D, D), lambda i: (0, 0)), vec, vec, vec, vec, vec],
        out_specs=row,
        out_shape=jax.ShapeDtypeStruct((N_TOK, D), F32),
        compiler_params=_cparams(("parallel",)),
        name="conv_out",
    )(u, x, mods_l, w_bf, b.reshape(1, D), conv_ln_g.reshape(1, D), conv_ln_b.reshape(1, D),
      ln_g.reshape(1, D), ln_b.reshape(1, D))


def _attn_out(o_ctx, o_lat, x, mods_l, w_bf, ln_g, ln_b):
    tm = TM_OUT
    row, vec, mod = _row_specs(tm)
    n_ctx = N_P // tm
    return pl.pallas_call(
        functools.partial(_attn_out_kernel, tm=tm),
        grid=(N_TOK // tm,),
        in_specs=[pl.BlockSpec((tm, D), lambda i: (jnp.minimum(i, n_ctx - 1), 0)),
                  pl.BlockSpec((tm, D), lambda i: (jnp.maximum(i - n_ctx, 0), 0)),
                  row, mod, pl.BlockSpec((D, D), lambda i: (0, 0)), vec, vec],
        out_specs=row,
        out_shape=jax.ShapeDtypeStruct((N_TOK, D), F32),
        compiler_params=_cparams(("parallel",)),
        name="attn_out",
    )(o_ctx, o_lat, x, mods_l, w_bf, ln_g.reshape(1, D), ln_b.reshape(1, D))


def _qkv_kernel(x_ref, mod_ref, w_ref, cos_ref, se_ref, so_ref, o_ref, h_ref):
    i, s = pl.program_id(0), pl.program_id(1)

    @pl.when(s == 0)
    def _():
        h_ref[...] = (x_ref[...] * (1.0 + mod_ref[1:2, :]) + mod_ref[0:1, :]).astype(BF16)

    rotate = (i * TM >= N_P) & (s < 2)
    chunks = [slice(c * TN, (c + 1) * TN) for c in range(D // TN)]

    @pl.when(rotate)
    def _():
        for cols in chunks:
            y = _bdot(h_ref[...], w_ref[:, cols])
            nxt = pltpu.roll(y, TN - 1, 1)
            prv = pltpu.roll(y, 1, 1)
            o_ref[:, cols] = y * cos_ref[...] + nxt * se_ref[...] + prv * so_ref[...]

    @pl.when(jnp.logical_not(rotate))
    def _():
        for cols in chunks:
            o_ref[:, cols] = _bdot(h_ref[...], w_ref[:, cols])


def _rope_tables():
    rows = SEQ_S // GRID_W
    row = jnp.repeat(jnp.arange(rows, dtype=F32), GRID_W)
    col = jnp.tile(jnp.arange(GRID_W, dtype=F32), rows)
    n_freq = HEAD_DIM // 4
    inv = ROPE_BASE ** (-jnp.arange(n_freq, dtype=F32) / n_freq)
    ang = jnp.concatenate([row[:, None] * inv, col[:, None] * inv], -1)
    cos = jnp.repeat(jnp.cos(ang), 2, axis=-1)
    sin = jnp.repeat(jnp.sin(ang), 2, axis=-1)
    even = (jnp.arange(HEAD_DIM) % 2 == 0)[None, :]
    sin_even = jnp.where(even, -sin, 0.0)
    sin_odd = jnp.where(even, 0.0, sin)
    reps = TN // HEAD_DIM
    return tuple(jnp.tile(t, (1, reps)) for t in (cos, sin_even, sin_odd))


def _qkv_proj(x, mods_l, w_qkv_bf, rope):
    tab = pl.BlockSpec(
        (TM, TN), lambda i, s: (jnp.where(i * TM >= N_P, ((i * TM - N_P) % SEQ_S) // TM, 0), 0))
    return pl.pallas_call(
        _qkv_kernel,
        grid=(N_TOK // TM, 3),
        in_specs=[
            pl.BlockSpec((TM, D), lambda i, s: (i, 0)),
            pl.BlockSpec((None, 6, D), lambda i, s: (_mod_row(i, TM), 0, 0)),
            pl.BlockSpec((None, D, D), lambda i, s: (s, 0, 0)),
            tab, tab, tab,
        ],
        out_specs=pl.BlockSpec((None, TM, D), lambda i, s: (s, i, 0)),
        out_shape=jax.ShapeDtypeStruct((3, N_TOK, D), F32),
        scratch_shapes=[pltpu.VMEM((TM, D), BF16)],
        compiler_params=_cparams(("parallel", "arbitrary")),
        name="qkv_proj",
    )(x, mods_l, w_qkv_bf, *rope)


def _softmax_parts(q_half, ks):
    dn = (((1,), (1,)), ((), ()))
    ss = [lax.dot_general(q_half, k, dn, preferred_element_type=F32) for k in ks]
    m = ss[0].max(axis=-1, keepdims=True)
    for s in ss[1:]:
        m = jnp.maximum(m, s.max(axis=-1, keepdims=True))
    ps = [jnp.exp(s - m) for s in ss]
    l = ps[0].sum(axis=-1, keepdims=True)
    for p in ps[1:]:
        l = l + p.sum(axis=-1, keepdims=True)
    return ps, 1.0 / l


def _attn_kernel(*refs, lam_init, has_cache, heads):
    if has_cache:
        lam_ref, g_ref, q_ref, k_ref, v_ref, kc_ref, vc_ref, o_ref = refs
    else:
        lam_ref, g_ref, q_ref, k_ref, v_ref, o_ref = refs
    lp = lam_ref[...]
    lam = (jnp.exp(jnp.sum(lp[0:1] * lp[1:2], axis=-1, keepdims=True))
           - jnp.exp(jnp.sum(lp[2:3] * lp[3:4], axis=-1, keepdims=True)) + lam_init)
    lane = lax.broadcasted_iota(I32, (q_ref.shape[0], HEAD_W), 1)
    for hd in range(heads):
        cols = slice(hd * HEAD_W, (hd + 1) * HEAD_W)
        q = q_ref[:, cols] * (HEAD_DIM ** -0.5)
        q1 = jnp.where(lane < HEAD_DIM, q, 0.0).astype(BF16)
        q2 = jnp.where(lane >= HEAD_DIM, q, 0.0).astype(BF16)
        ks = [k_ref[:, cols].astype(BF16)]
        vs = [v_ref[:, cols].astype(BF16)]
        if has_cache:
            ks.append(kc_ref[:, cols].astype(BF16))
            vs.append(vc_ref[:, cols].astype(BF16))
        p1, r1 = _softmax_parts(q1, ks)
        p2, r2 = _softmax_parts(q2, ks)
        r2 = lam * r2
        o = None
        for a1, a2, v in zip(p1, p2, vs):
            a = (a1 * r1 - a2 * r2).astype(BF16)
            t = _bdot(a, v)
            o = t if o is None else o + t
        ms = jnp.mean(o * o, axis=-1, keepdims=True)
        o_ref[:, cols] = o * lax.rsqrt(ms + LN_EPS) * g_ref[...] * (1.0 - lam_init)


def _attention(qkv, lam_params, subln_g, cache_k, cache_v, lam_init):
    g2 = subln_g.reshape(1, HEAD_W)
    small = [pl.BlockSpec((4, HEAD_DIM), lambda *_: (0, 0)), pl.BlockSpec((1, HEAD_W), lambda *_: (0, 0))]
    wc = HEADS_CTX * HEAD_W
    o_ctx = pl.pallas_call(
        functools.partial(_attn_kernel, lam_init=lam_init, has_cache=False, heads=HEADS_CTX),
        grid=(B_P, N_HEADS // HEADS_CTX),
        in_specs=small + [
            pl.BlockSpec((None, SEQ_P, wc), lambda b, h: (0, b, h)),
            pl.BlockSpec((None, SEQ_P, wc), lambda b, h: (1, b, h)),
            pl.BlockSpec((None, SEQ_P, wc), lambda b, h: (2, b, h)),
        ],
        out_specs=pl.BlockSpec((SEQ_P, wc), lambda b, h: (b, h)),
        out_shape=jax.ShapeDtypeStruct((N_P, D), F32),
        compiler_params=_cparams(("parallel", "parallel")),
        name="attn_ctx",
    )(lam_params, g2, qkv, qkv, qkv)
    nq = SEQ_S // TQ
    q0 = N_P // TQ
    s0 = N_P // SEQ_S
    wl = HEADS_LAT * HEAD_W
    o_lat = pl.pallas_call(
        functools.partial(_attn_kernel, lam_init=lam_init, has_cache=True, heads=HEADS_LAT),
        grid=(B_S, N_HEADS // HEADS_LAT, nq),
        in_specs=small + [
            pl.BlockSpec((None, TQ, wl), lambda b, h, t: (0, q0 + b * nq + t, h)),
            pl.BlockSpec((None, SEQ_S, wl), lambda b, h, t: (1, s0 + b, h)),
            pl.BlockSpec((None, SEQ_S, wl), lambda b, h, t: (2, s0 + b, h)),
            pl.BlockSpec((PAST, wl), lambda b, h, t: (b, h)),
            pl.BlockSpec((PAST, wl), lambda b, h, t: (b, h)),
        ],
        out_specs=pl.BlockSpec((TQ, wl), lambda b, h, t: (b * nq + t, h)),
        out_shape=jax.ShapeDtypeStruct((N_S, D), F32),
        compiler_params=_cparams(("parallel", "parallel", "arbitrary")),
        name="attn_lat",
    )(lam_params, g2, qkv, qkv, qkv, cache_k.reshape(B_S * PAST, D), cache_v.reshape(B_S * PAST, D))
    return o_ctx, o_lat


def _router_kernel(x_ref, mod_ref, w_ref, b_ref, h_ref, r_ref, cnt_ref, carry_ref, *, tm):
    @pl.when(pl.program_id(0) == 0)
    def _():
        carry_ref[...] = jnp.zeros_like(carry_ref)

    h = x_ref[...] * (1.0 + mod_ref[4:5, :]) + mod_ref[3:4, :]
    for c in range(SLAB):
        h_ref[pl.ds(c, tm, stride=SLAB), :] = h[:, c * LANES:(c + 1) * LANES]
    hh = h.astype(BF16)
    hl = (h - hh.astype(F32)).astype(BF16)
    w = w_ref[...]
    t = _bdot(hh, w)
    logit = t[:, :LANES] + t[:, LANES:] + _bdot(hl, w[:, :LANES]) + b_ref[...]
    lane = lax.broadcasted_iota(I32, logit.shape, 1).astype(F32)
    neg = -jnp.inf
    big = 4.0 * LANES
    lg = jnp.where(lane < N_GROUPS, logit, neg)
    gmax = lg.max(axis=-1, keepdims=True)
    grp = jnp.where(lg == gmax, lane, big).min(axis=-1, keepdims=True)
    pg = 1.0 / jnp.exp(lg - gmax).sum(axis=-1, keepdims=True)
    lo = N_GROUPS + grp * EPG
    le = jnp.where((lane >= lo) & (lane < lo + EPG), logit, neg)
    v1 = le.max(axis=-1, keepdims=True)
    i1 = jnp.where(le == v1, lane, big).min(axis=-1, keepdims=True)
    le2 = jnp.where(lane == i1, neg, le)
    v2 = le2.max(axis=-1, keepdims=True)
    i2 = jnp.where(le2 == v2, lane, big).min(axis=-1, keepdims=True)
    e = jnp.exp(v2 - v1)
    den = 1.0 / (1.0 + e)
    w1 = pg * den
    w2 = pg * (e * den)
    e1 = i1 - N_GROUPS
    e2 = i2 - N_GROUPS
    r_ref[...] = jnp.where(lane == 0, e1, jnp.where(lane == 1, e2, jnp.where(lane == 2, w1, jnp.where(lane == 3, w2, 0.0))))
    picked = jnp.where(lane == e1, 1.0, jnp.where(lane == e2, 1.0, 0.0))
    carry_ref[...] = carry_ref[...] + picked.sum(axis=0, keepdims=True)
    cnt_ref[...] = jnp.broadcast_to(carry_ref[...], cnt_ref.shape)


def _router(x, mods_l, w_hl, b_r):
    tm = TM_LN
    return pl.pallas_call(
        functools.partial(_router_kernel, tm=tm),
        grid=(N_TOK // tm,),
        in_specs=[
            pl.BlockSpec((tm, D), lambda i: (i, 0)),
            pl.BlockSpec((None, 6, D), lambda i: (_mod_row(i, tm), 0, 0)),
            pl.BlockSpec((D, 2 * LANES), lambda i: (0, 0)),
            pl.BlockSpec((1, LANES), lambda i: (0, 0)),
        ],
        out_specs=[pl.BlockSpec((tm * SLAB, LANES), lambda i: (i, 0)),
                   pl.BlockSpec((tm, LANES), lambda i: (i, 0)),
                   pl.BlockSpec((SUBLANES, LANES), lambda i: (0, 0))],
        out_shape=[jax.ShapeDtypeStruct((N_TOK * SLAB, LANES), F32),
                   jax.ShapeDtypeStruct((N_TOK, LANES), F32),
                   jax.ShapeDtypeStruct((SUBLANES, LANES), F32)],
        scratch_shapes=[pltpu.VMEM((1, LANES), F32)],
        compiler_params=_cparams(("arbitrary",)),
        name="router",
    )(x, mods_l, w_hl, b_r)


def _slot_kernel(r_ref, start_ref, slot_ref, carry_ref):
    @pl.when(pl.program_id(0) == 0)
    def _():
        carry_ref[...] = jnp.zeros_like(carry_ref)

    r = r_ref[...]
    lane = lax.broadcasted_iota(I32, r.shape, 1)
    e1 = r[:, 0:1].astype(I32)
    e2 = r[:, 1:2].astype(I32)
    oh1 = (lane == e1).astype(F32)
    oh2 = (lane == e2).astype(F32)
    oh = oh1 + oh2
    rr = lax.broadcasted_iota(I32, (TR, TR), 0)
    cc = lax.broadcasted_iota(I32, (TR, TR), 1)
    tri = (cc < rr).astype(BF16)
    before = _bdot(tri, oh.astype(BF16)) + carry_ref[...] + start_ref[...]
    k1 = (oh1 * before).sum(axis=-1, keepdims=True)
    k2 = (oh2 * before).sum(axis=-1, keepdims=True)
    slot_ref[...] = jnp.where(lane == 0, k1, jnp.where(lane == 1, k2, 0.0))
    carry_ref[...] = carry_ref[...] + oh.sum(axis=0, keepdims=True)


def _slots(route, start):
    return pl.pallas_call(
        _slot_kernel,
        grid=(N_TOK // TR,),
        in_specs=[pl.BlockSpec((TR, LANES), lambda i: (i, 0)), pl.BlockSpec((1, LANES), lambda i: (0, 0))],
        out_specs=pl.BlockSpec((TR, LANES), lambda i: (i, 0)),
        out_shape=jax.ShapeDtypeStruct((N_TOK, LANES), F32),
        scratch_shapes=[pltpu.VMEM((1, LANES), F32)],
        compiler_params=_cparams(("arbitrary",)),
        name="expert_slots",
    )(route, start)


def _row_copy_waits(n, wait_rows):
    p = TB
    while p >= 1:
        @pl.when((n & p) != 0)
        def _(p=p):
            wait_rows(p)
        p //= 2


def _experts_kernel(cnt_ref, blk0_ref, tok_ref, dst_ref,
                    h_hbm, wg_hbm, wu_hbm, wd_hbm,
                    out_hbm,
                    wg_f, wu_f, wd_f, wg_s, wu_s, wd_s, xbuf, x2d, ybuf, st, wsem, gsem, ssem, *, layer):
    e = pl.program_id(0)
    cnt = cnt_ref[e]
    nblk = (cnt + TB - 1) // TB
    slot0 = blk0_ref[e] * TB

    def weight_copies(ex, wbuf):
        return [pltpu.make_async_copy(src.at[layer, ex], dst.at[wbuf], wsem.at[wbuf, k])
                for k, (src, dst) in enumerate(((wg_hbm, wg_f), (wu_hbm, wu_f), (wd_hbm, wd_f)))]

    @pl.when(e == 0)
    def _():
        for cp in weight_copies(0, 0):
            cp.start(priority=1)

    @pl.when(e + 1 < N_EXPERTS)
    def _():
        for cp in weight_copies(e + 1, (e + 1) % 2):
            cp.start(priority=1)

    @pl.when(e == 0)
    def _():
        xbuf[...] = jnp.zeros_like(xbuf)

    def rows_in(b):
        return jnp.minimum(cnt - b * TB, TB)

    def slab_rows(i, n):
        start = i * SLAB
        return pl.ds(start if isinstance(i, int) else pl.multiple_of(start, SLAB), n * SLAB)

    def gather_desc(buf, r, t, n):
        return pltpu.make_async_copy(h_hbm.at[slab_rows(t, n)], xbuf.at[buf, slab_rows(r, n)], gsem.at[buf])

    def scatter_desc(r, d, n):
        return pltpu.make_async_copy(ybuf.at[slab_rows(r, n)], out_hbm.at[slab_rows(d, n)], ssem)

    def start_gather(base, n_rows, buf):
        def issue(r, carry):
            gather_desc(buf, r, tok_ref[base + r], 1).start()
            return carry

        lax.fori_loop(0, n_rows, issue, 0)

    def start_scatter(base, n_rows):
        def issue(r, carry):
            scatter_desc(r, dst_ref[base + r], 1).start()
            return carry

        lax.fori_loop(0, n_rows, issue, 0)
        st[0] = n_rows

    def wait_scatter():
        _row_copy_waits(st[0], lambda n: scatter_desc(0, 0, n).wait())
        st[0] = 0

    @pl.when(e == 0)
    def _():
        st[0] = 0
        st[1] = 0
        st[2] = 0

    par = st[1]
    nxt = jnp.minimum(e + 1, N_EXPERTS - 1)
    nxt_cnt = jnp.where(e + 1 < N_EXPERTS, cnt_ref[nxt], 0)

    @pl.when((nblk > 0) & (st[2] == 0))
    def _():
        start_gather(slot0, rows_in(0), par)

    wbuf = e % 2
    for cp in weight_copies(e, wbuf):
        cp.wait()

    @pl.when(nblk > 0)
    def _():
        wg_s[...] = wg_f[wbuf].astype(BF16)
        wu_s[...] = wu_f[wbuf].astype(BF16)
        wd_s[...] = wd_f[wbuf].astype(BF16)

        def block(b, carry):
            buf = (par + b) % 2

            @pl.when(b + 1 < nblk)
            def _():
                start_gather(slot0 + (b + 1) * TB, rows_in(b + 1), 1 - buf)

            @pl.when((b + 1 == nblk) & (nxt_cnt > 0))
            def _():
                start_gather(blk0_ref[nxt] * TB, jnp.minimum(nxt_cnt, TB), 1 - buf)

            _row_copy_waits(rows_in(b), lambda n: gather_desc(buf, 0, 0, n).wait())
            for c in range(SLAB):
                x2d[:, c * LANES:(c + 1) * LANES] = xbuf[buf, pl.ds(c, TB, stride=SLAB), :].astype(BF16)
            x = x2d[...]
            g = _bdot(x, wg_s[...])
            u = _bdot(x, wu_s[...])
            y = _bdot((_silu(g) * u).astype(BF16), wd_s[...])
            wait_scatter()
            for c in range(SLAB):
                ybuf[pl.ds(c, TB, stride=SLAB), :] = y[:, c * LANES:(c + 1) * LANES]
            start_scatter(slot0 + b * TB, rows_in(b))
            return carry

        lax.fori_loop(0, nblk, block, 0)
        st[1] = (par + nblk) % 2
        st[2] = jnp.where(nxt_cnt > 0, 1, 0)

    @pl.when(nblk == 0)
    def _():
        st[2] = 0

    @pl.when(e == N_EXPERTS - 1)
    def _():
        wait_scatter()


def _experts(h_slab, cnt, blk0, slot_tok, slot_dst, w_gate, w_up, w_down, layer):
    hbm = pl.BlockSpec(memory_space=pl.ANY)
    return pl.pallas_call(
        functools.partial(_experts_kernel, layer=layer),
        grid_spec=pltpu.PrefetchScalarGridSpec(
            num_scalar_prefetch=4,
            grid=(N_EXPERTS,),
            in_specs=[hbm, hbm, hbm, hbm],
            out_specs=hbm,
            scratch_shapes=[pltpu.VMEM((2, D, D_EXPERT), F32), pltpu.VMEM((2, D, D_EXPERT), F32),
                            pltpu.VMEM((2, D_EXPERT, D), F32),
                            pltpu.VMEM((D, D_EXPERT), BF16), pltpu.VMEM((D, D_EXPERT), BF16),
                            pltpu.VMEM((D_EXPERT, D), BF16),
                            pltpu.VMEM((2, TB * SLAB, LANES), F32),
                            pltpu.VMEM((TB, D), BF16),
                            pltpu.VMEM((TB * SLAB, LANES), F32),
                            pltpu.SMEM((4,), I32),
                            pltpu.SemaphoreType.DMA((2, 3)),
                            pltpu.SemaphoreType.DMA((2,)),
                            pltpu.SemaphoreType.DMA(())],
        ),
        out_shape=jax.ShapeDtypeStruct((TOP_K * N_TOK * SLAB, LANES), F32),
        compiler_params=_cparams(("arbitrary",)),
        name="experts",
    )(cnt, blk0, slot_tok, slot_dst, h_slab, w_gate, w_up, w_down)


def _ffn_out_kernel(x_ref, r0_ref, r1_ref, route_ref, mod_ref, lg_ref, lb_ref, o_ref, *, tm):
    rt = route_ref[...]
    w0, w1 = rt[:, 2:3], rt[:, 3:4]
    f = jnp.concatenate(
        [r0_ref[pl.ds(c, tm, stride=SLAB), :] * w0 + r1_ref[pl.ds(c, tm, stride=SLAB), :] * w1
         for c in range(SLAB)], axis=1)
    z = ALPHA * x_ref[...] + mod_ref[5:6, :] * f
    o_ref[...] = _ln(z, lg_ref[...], lb_ref[...])


def _ffn_out(x, picked, route, mods_l, ln_g, ln_b):
    tm = TM_LN
    nb = N_TOK // tm
    row, vec, mod = _row_specs(tm)
    return pl.pallas_call(
        functools.partial(_ffn_out_kernel, tm=tm),
        grid=(nb,),
        in_specs=[row,
                  pl.BlockSpec((tm * SLAB, LANES), lambda i: (i, 0)),
                  pl.BlockSpec((tm * SLAB, LANES), lambda i: (i + nb, 0)),
                  pl.BlockSpec((tm, LANES), lambda i: (i, 0)),
                  mod, vec, vec],
        out_specs=row,
        out_shape=jax.ShapeDtypeStruct((N_TOK, D), F32),
        compiler_params=_cparams(("parallel",)),
        name="ffn_out",
    )(x, picked, picked, route, mods_l, ln_g.reshape(1, D), ln_b.reshape(1, D))


def _moe(x, mods_l, layer, w_rg, b_rg, w_re, b_re, w_gate, w_up, w_down, ln_g, ln_b):
    w_r = jnp.zeros((D, LANES), F32).at[:, :N_GROUPS].set(w_rg).at[:, N_GROUPS:N_GROUPS + N_EXPERTS].set(w_re)
    w_hi = w_r.astype(BF16)
    w_lo = (w_r - w_hi.astype(F32)).astype(BF16)
    b_r = jnp.zeros((1, LANES), F32).at[0, :N_GROUPS].set(b_rg).at[0, N_GROUPS:N_GROUPS + N_EXPERTS].set(b_re)
    h_slab, route, counts = _router(x, mods_l, jnp.concatenate([w_hi, w_lo], axis=1), b_r)
    cnt = counts[0, :N_EXPERTS].astype(I32)
    nblk = (cnt + TB - 1) // TB
    blk0 = jnp.cumsum(nblk) - nblk
    start = jnp.zeros((1, LANES), F32).at[0, :N_EXPERTS].set((blk0 * TB).astype(F32))
    slot = _slots(route, start)[:, :TOP_K].astype(I32)
    dst = jnp.arange(TOP_K, dtype=I32)[None, :] * N_TOK + jnp.arange(N_TOK, dtype=I32)[:, None]
    slot_dst = jnp.zeros((N_SLOTS,), I32).at[slot.reshape(-1)].set(dst.reshape(-1))
    slot_tok = slot_dst % N_TOK
    picked = _experts(h_slab, cnt, blk0.astype(I32), slot_tok, slot_dst, w_gate, w_up, w_down, layer)
    return _ffn_out(x, picked, route, mods_l, ln_g, ln_b)


def kernel(x_prompt, x_sample, c, cache_k_l1, cache_v_l1, cache_k_l3, cache_v_l3, c_ctx, w_mod, b_mod, ln_mix_g, ln_mix_b, ln_ffn_g, ln_ffn_b, conv_w_pw1, conv_b_pw1, conv_w_dw, conv_b_dw, conv_ln_g, conv_ln_b, conv_w_pw2, conv_b_pw2, attn_w_q, attn_w_k, attn_w_v, attn_w_o, attn_lambda_q1, attn_lambda_k1, attn_lambda_q2, attn_lambda_k2, attn_subln_g, router_w_group, router_b_group, router_w_expert, router_b_expert, moe_w_gate, moe_w_up, moe_w_down):
    caches = ((cache_k_l1, cache_v_l1), (cache_k_l3, cache_v_l3))
    x = jnp.concatenate([x_prompt.reshape(N_P, D), x_sample.reshape(N_S, D)], axis=0)
    cond = jnp.zeros((N_MOD_ROWS, D), F32).at[0].set(c_ctx).at[1:1 + B_S].set(c)
    mods = _modulation(cond, w_mod, b_mod).reshape(DEPTH, N_MOD_ROWS, 6, D)
    rope = _rope_tables()
    new_kv = []
    for i in range(DEPTH):
        j = i // 2
        mods_l = mods[i]
        if i % 2 == 0:
            u = _pw1_glu(x, mods_l, conv_w_pw1[j].astype(BF16), conv_b_pw1[j])
            u = _dwconv(u, conv_w_dw[j], conv_b_dw[j])
            x = _conv_out(u, x, mods_l, conv_w_pw2[j].astype(BF16), conv_b_pw2[j], conv_ln_g[j], conv_ln_b[j],
                          ln_mix_g[i], ln_mix_b[i])
        else:
            lam_init = 0.8 - 0.6 * math.exp(-0.3 * i)
            w_qkv = jnp.stack([attn_w_q[j], attn_w_k[j], attn_w_v[j]]).astype(BF16)
            qkv = _qkv_proj(x, mods_l, w_qkv, rope)
            new_kv.append(qkv[1, :N_P].reshape(B_P, SEQ_P, N_HEADS, HEAD_W))
            new_kv.append(qkv[2, :N_P].reshape(B_P, SEQ_P, N_HEADS, HEAD_W))
            lam_params = jnp.stack([attn_lambda_q1[j], attn_lambda_k1[j], attn_lambda_q2[j], attn_lambda_k2[j]])
            o_ctx, o_lat = _attention(qkv, lam_params, attn_subln_g[j], caches[j][0], caches[j][1], lam_init)
            x = _attn_out(o_ctx, o_lat, x, mods_l, attn_w_o[j].astype(BF16), ln_mix_g[i], ln_mix_b[i])
        x = _moe(x, mods_l, i, router_w_group[i], router_b_group[i], router_w_expert[i], router_b_expert[i],
                 moe_w_gate, moe_w_up, moe_w_down, ln_ffn_g[i], ln_ffn_b[i])
    return (x[:N_P].reshape(B_P, SEQ_P, D), x[N_P:].reshape(B_S, SEQ_S, D),
            new_kv[0], new_kv[1], new_kv[2], new_kv[3])
```

```python
import functools
import math

import jax
import jax.numpy as jnp
from jax import lax
from jax.experimental import pallas as pl
from jax.experimental.pallas import tpu as pltpu

F32 = jnp.float32
BF16 = jnp.bfloat16
I32 = jnp.int32

D = 2048
DEPTH = 4
B_P, SEQ_P = 16, 256
B_S, SEQ_S = 8, 1024
PAST = 256
N_P = B_P * SEQ_P
N_S = B_S * SEQ_S
N_TOK = N_P + N_S
GRID_W = 64
HEAD_DIM = 64
N_HEADS = D // (2 * HEAD_DIM)
HEAD_W = 2 * HEAD_DIM
ROPE_BASE = 10000.0
CONV_W = 31
CONV_HALO = 16
N_GROUPS = 8
EPG = 8
N_EXPERTS = N_GROUPS * EPG
TOP_K = 2
D_EXPERT = D // 4
ALPHA = (2.0 * DEPTH) ** 0.25
LN_EPS = 1e-5
N_MOD_ROWS = 16

LANES = 128
SUBLANES = 8
SLAB = D // LANES
VMEM_LIMIT = 56 * 1024 * 1024

TM = 512
TN = 512
TN_PW1 = 1024
TM_OUT = 512
TM_LN = 256
TC_CONV = 512
TQ = 256
HEADS_CTX = 16
HEADS_LAT = 4
TB = 128
N_SLOTS = N_TOK * TOP_K + N_EXPERTS * TB
N_BLK = N_SLOTS // TB
TR = 512
GATHER_CHUNK = 1024


def _cparams(sem):
    return pltpu.CompilerParams(dimension_semantics=sem, vmem_limit_bytes=VMEM_LIMIT)


def _mod_row(i, tm):
    t0 = i * tm
    return jnp.where(t0 < N_P, 0, (t0 - N_P) // SEQ_S + 1)


def _ln(z, g, b):
    mu = jnp.mean(z, axis=-1, keepdims=True)
    zc = z - mu
    var = jnp.mean(zc * zc, axis=-1, keepdims=True)
    return zc * lax.rsqrt(var + LN_EPS) * g + b


def _silu(x):
    return x * jax.nn.sigmoid(x)


def _bdot(a, b):
    return jnp.dot(a, b, preferred_element_type=F32)


def _mod_kernel(cond_ref, w_ref, b_ref, o_ref):
    s = _silu(cond_ref[...]).astype(BF16)
    o_ref[...] = _bdot(s, w_ref[...].astype(BF16)) + b_ref[...]


def _modulation(cond, w_mod, b_mod):
    tn = 1024
    n_out = 6 * D
    return pl.pallas_call(
        _mod_kernel,
        grid=(DEPTH, n_out // tn),
        in_specs=[
            pl.BlockSpec((N_MOD_ROWS, D), lambda l, j: (0, 0)),
            pl.BlockSpec((None, D, tn), lambda l, j: (l, 0, j)),
            pl.BlockSpec((None, 1, tn), lambda l, j: (l, 0, j)),
        ],
        out_specs=pl.BlockSpec((None, N_MOD_ROWS, tn), lambda l, j: (l, 0, j)),
        out_shape=jax.ShapeDtypeStruct((DEPTH, N_MOD_ROWS, n_out), F32),
        compiler_params=_cparams(("parallel", "parallel")),
        name="modulation",
    )(cond, w_mod, b_mod.reshape(DEPTH, 1, n_out))


def _pw1_kernel(x_ref, mod_ref, wa_ref, wg_ref, ba_ref, bg_ref, o_ref, h_ref):
    @pl.when(pl.program_id(1) == 0)
    def _():
        h_ref[...] = (x_ref[...] * (1.0 + mod_ref[1:2, :]) + mod_ref[0:1, :]).astype(BF16)

    h = h_ref[...]
    for c in range(TN_PW1 // TN):
        cols = slice(c * TN, (c + 1) * TN)
        a = _bdot(h, wa_ref[:, cols]) + ba_ref[:, cols]
        g = _bdot(h, wg_ref[:, cols]) + bg_ref[:, cols]
        o_ref[:, cols] = a * jax.nn.sigmoid(g)


def _pw1_glu(x, mods_l, w_bf, b):
    nj = D // TN_PW1
    b2 = b.reshape(1, 2 * D)
    return pl.pallas_call(
        _pw1_kernel,
        grid=(N_TOK // TM, nj),
        in_specs=[
            pl.BlockSpec((TM, D), lambda i, j: (i, 0)),
            pl.BlockSpec((None, 6, D), lambda i, j: (_mod_row(i, TM), 0, 0)),
            pl.BlockSpec((D, TN_PW1), lambda i, j: (0, j)),
            pl.BlockSpec((D, TN_PW1), lambda i, j: (0, j + nj)),
            pl.BlockSpec((1, TN_PW1), lambda i, j: (0, j)),
            pl.BlockSpec((1, TN_PW1), lambda i, j: (0, j + nj)),
        ],
        out_specs=pl.BlockSpec((TM, TN_PW1), lambda i, j: (i, j)),
        out_shape=jax.ShapeDtypeStruct((N_TOK, D), F32),
        scratch_shapes=[pltpu.VMEM((TM, D), BF16)],
        compiler_params=_cparams(("parallel", "arbitrary")),
        name="pw1_glu",
    )(x, mods_l, w_bf, w_bf, b2, b2)


def _dwconv_kernel(main_ref, top_ref, bot_ref, w_ref, b_ref, o_ref, win_ref, *, tm):
    i = pl.program_id(0)
    t0 = i * tm
    seq = jnp.where(t0 < N_P, SEQ_P, SEQ_S)
    pos = jnp.where(t0 < N_P, t0 % SEQ_P, (t0 - N_P) % SEQ_S)
    has_top = pos > 0
    has_bot = pos + tm < seq
    win_ref[0, 0:CONV_HALO, :] = jnp.where(has_top, top_ref[...], 0.0)
    win_ref[0, CONV_HALO:CONV_HALO + tm, :] = main_ref[...]
    win_ref[0, CONV_HALO + tm:, :] = jnp.where(has_bot, bot_ref[...], 0.0)
    span = tm + 2 * CONV_HALO - SUBLANES
    for s in range(1, SUBLANES):
        win_ref[s, 0:span, :] = win_ref[0, s:s + span, :]
    rs = 32
    off = CONV_HALO - CONV_W // 2
    for r0 in range(0, tm, rs):
        acc = jnp.zeros((rs, o_ref.shape[1]), F32)
        for k in range(CONV_W):
            a, s = divmod(off + k, SUBLANES)
            acc = acc + win_ref[s, r0 + a * SUBLANES:r0 + a * SUBLANES + rs, :] * w_ref[k:k + 1, :]
        o_ref[r0:r0 + rs, :] = acc + b_ref[...]


def _dwconv(u, w_dw, b_dw):
    tm, tc = TM_LN, TC_CONV
    hb = tm // CONV_HALO
    n_hblk = N_TOK // CONV_HALO
    w_pad = jnp.zeros((32, D), F32).at[:CONV_W].set(w_dw)
    return pl.pallas_call(
        functools.partial(_dwconv_kernel, tm=tm),
        grid=(N_TOK // tm, D // tc),
        in_specs=[
            pl.BlockSpec((tm, tc), lambda i, c: (i, c)),
            pl.BlockSpec((CONV_HALO, tc), lambda i, c: (jnp.maximum(i * hb - 1, 0), c)),
            pl.BlockSpec((CONV_HALO, tc), lambda i, c: (jnp.minimum((i + 1) * hb, n_hblk - 1), c)),
            pl.BlockSpec((32, tc), lambda i, c: (0, c)),
            pl.BlockSpec((1, tc), lambda i, c: (0, c)),
        ],
        out_specs=pl.BlockSpec((tm, tc), lambda i, c: (i, c)),
        out_shape=jax.ShapeDtypeStruct((N_TOK, D), F32),
        scratch_shapes=[pltpu.VMEM((SUBLANES, tm + 2 * CONV_HALO, tc), F32)],
        compiler_params=_cparams(("parallel", "parallel")),
        name="dwconv",
    )(u, u, u, w_pad, b_dw.reshape(1, D))


def _deepnorm(x, gate, m, g, b):
    return _ln(ALPHA * x + gate * m, g, b)


def _row_halves(tm):
    return [slice(r, r + TM_LN) for r in range(0, tm, TM_LN)]


def _conv_out_kernel(u_ref, x_ref, mod_ref, w_ref, b_ref, cg_ref, cb_ref, lg_ref, lb_ref, o_ref, *, tm):
    for rows in _row_halves(tm):
        u = _silu(_ln(u_ref[rows, :], cg_ref[...], cb_ref[...]))
        m = _bdot(u.astype(BF16), w_ref[...]) + b_ref[...]
        o_ref[rows, :] = _deepnorm(x_ref[rows, :], mod_ref[2:3, :], m, lg_ref[...], lb_ref[...])


def _attn_out_kernel(op_ref, os_ref, x_ref, mod_ref, w_ref, lg_ref, lb_ref, o_ref, *, tm):
    is_ctx = pl.program_id(0) * tm < N_P
    for rows in _row_halves(tm):
        o = jnp.where(is_ctx, op_ref[rows, :], os_ref[rows, :])
        m = _bdot(o.astype(BF16), w_ref[...])
        o_ref[rows, :] = _deepnorm(x_ref[rows, :], mod_ref[2:3, :], m, lg_ref[...], lb_ref[...])


def _row_specs(tm):
    row = pl.BlockSpec((tm, D), lambda i: (i, 0))
    vec = pl.BlockSpec((1, D), lambda i: (0, 0))
    mod = pl.BlockSpec((None, 6, D), lambda i: (_mod_row(i, tm), 0, 0))
    return row, vec, mod


def _conv_out(u, x, mods_l, w_bf, b, conv_ln_g, conv_ln_b, ln_g, ln_b):
    tm = TM_OUT
    row, vec, mod = _row_specs(tm)
    return pl.pallas_call(
        functools.partial(_conv_out_kernel, tm=tm),
        grid=(N_TOK // tm,),
        in_specs=[row, row, mod, pl.BlockSpec((D, D), lambda i: (0, 0)), vec, vec, vec, vec, vec],
        out_specs=row,
        out_shape=jax.ShapeDtypeStruct((N_TOK, D), F32),
        compiler_params=_cparams(("parallel",)),
        name="conv_out",
    )(u, x, mods_l, w_bf, b.reshape(1, D), conv_ln_g.reshape(1, D), conv_ln_b.reshape(1, D),
      ln_g.reshape(1, D), ln_b.reshape(1, D))


def _attn_out(o_ctx, o_lat, x, mods_l, w_bf, ln_g, ln_b):
    tm = TM_OUT
    row, vec, mod = _row_specs(tm)
    n_ctx = N_P // tm
    return pl.pallas_call(
        functools.partial(_attn_out_kernel, tm=tm),
        grid=(N_TOK // tm,),
        in_specs=[pl.BlockSpec((tm, D), lambda i: (jnp.minimum(i, n_ctx - 1), 0)),
                  pl.BlockSpec((tm, D), lambda i: (jnp.maximum(i - n_ctx, 0), 0)),
                  row, mod, pl.BlockSpec((D, D), lambda i: (0, 0)), vec, vec],
        out_specs=row,
        out_shape=jax.ShapeDtypeStruct((N_TOK, D), F32),
        compiler_params=_cparams(("parallel",)),
        name="attn_out",
    )(o_ctx, o_lat, x, mods_l, w_bf, ln_g.reshape(1, D), ln_b.reshape(1, D))


def _qkv_kernel(x_ref, mod_ref, w_ref, cos_ref, se_ref, so_ref, o_ref, h_ref):
    i, s = pl.program_id(0), pl.program_id(1)

    @pl.when(s == 0)
    def _():
        h_ref[...] = (x_ref[...] * (1.0 + mod_ref[1:2, :]) + mod_ref[0:1, :]).astype(BF16)

    rotate = (i * TM >= N_P) & (s < 2)
    chunks = [slice(c * TN, (c + 1) * TN) for c in range(D // TN)]

    @pl.when(rotate)
    def _():
        for cols in chunks:
            y = _bdot(h_ref[...], w_ref[:, cols])
            nxt = pltpu.roll(y, TN - 1, 1)
            prv = pltpu.roll(y, 1, 1)
            o_ref[:, cols] = y * cos_ref[...] + nxt * se_ref[...] + prv * so_ref[...]

    @pl.when(jnp.logical_not(rotate))
    def _():
        for cols in chunks:
            o_ref[:, cols] = _bdot(h_ref[...], w_ref[:, cols])


def _rope_tables():
    rows = SEQ_S // GRID_W
    row = jnp.repeat(jnp.arange(rows, dtype=F32), GRID_W)
    col = jnp.tile(jnp.arange(GRID_W, dtype=F32), rows)
    n_freq = HEAD_DIM // 4
    inv = ROPE_BASE ** (-jnp.arange(n_freq, dtype=F32) / n_freq)
    ang = jnp.concatenate([row[:, None] * inv, col[:, None] * inv], -1)
    cos = jnp.repeat(jnp.cos(ang), 2, axis=-1)
    sin = jnp.repeat(jnp.sin(ang), 2, axis=-1)
    even = (jnp.arange(HEAD_DIM) % 2 == 0)[None, :]
    sin_even = jnp.where(even, -sin, 0.0)
    sin_odd = jnp.where(even, 0.0, sin)
    reps = TN // HEAD_DIM
    return tuple(jnp.tile(t, (1, reps)) for t in (cos, sin_even, sin_odd))


def _qkv_proj(x, mods_l, w_qkv_bf, rope):
    tab = pl.BlockSpec(
        (TM, TN), lambda i, s: (jnp.where(i * TM >= N_P, ((i * TM - N_P) % SEQ_S) // TM, 0), 0))
    return pl.pallas_call(
        _qkv_kernel,
        grid=(N_TOK // TM, 3),
        in_specs=[
            pl.BlockSpec((TM, D), lambda i, s: (i, 0)),
            pl.BlockSpec((None, 6, D), lambda i, s: (_mod_row(i, TM), 0, 0)),
            pl.BlockSpec((None, D, D), lambda i, s: (s, 0, 0)),
            tab, tab, tab,
        ],
        out_specs=pl.BlockSpec((None, TM, D), lambda i, s: (s, i, 0)),
        out_shape=jax.ShapeDtypeStruct((3, N_TOK, D), F32),
        scratch_shapes=[pltpu.VMEM((TM, D), BF16)],
        compiler_params=_cparams(("parallel", "arbitrary")),
        name="qkv_proj",
    )(x, mods_l, w_qkv_bf, *rope)


def _softmax_parts(q_half, ks):
    dn = (((1,), (1,)), ((), ()))
    ss = [lax.dot_general(q_half, k, dn, preferred_element_type=F32) for k in ks]
    m = ss[0].max(axis=-1, keepdims=True)
    for s in ss[1:]:
        m = jnp.maximum(m, s.max(axis=-1, keepdims=True))
    ps = [jnp.exp(s - m) for s in ss]
    l = ps[0].sum(axis=-1, keepdims=True)
    for p in ps[1:]:
        l = l + p.sum(axis=-1, keepdims=True)
    return ps, 1.0 / l


def _attn_kernel(*refs, lam_init, has_cache, heads):
    if has_cache:
        lam_ref, g_ref, q_ref, k_ref, v_ref, kc_ref, vc_ref, o_ref = refs
    else:
        lam_ref, g_ref, q_ref, k_ref, v_ref, o_ref = refs
    lp = lam_ref[...]
    lam = (jnp.exp(jnp.sum(lp[0:1] * lp[1:2], axis=-1, keepdims=True))
           - jnp.exp(jnp.sum(lp[2:3] * lp[3:4], axis=-1, keepdims=True)) + lam_init)
    lane = lax.broadcasted_iota(I32, (q_ref.shape[0], HEAD_W), 1)
    for hd in range(heads):
        cols = slice(hd * HEAD_W, (hd + 1) * HEAD_W)
        q = q_ref[:, cols] * (HEAD_DIM ** -0.5)
        q1 = jnp.where(lane < HEAD_DIM, q, 0.0).astype(BF16)
        q2 = jnp.where(lane >= HEAD_DIM, q, 0.0).astype(BF16)
        ks = [k_ref[:, cols].astype(BF16)]
        vs = [v_ref[:, cols].astype(BF16)]
        if has_cache:
            ks.append(kc_ref[:, cols].astype(BF16))
            vs.append(vc_ref[:, cols].astype(BF16))
        p1, r1 = _softmax_parts(q1, ks)
        p2, r2 = _softmax_parts(q2, ks)
        r2 = lam * r2
        o1 = o2 = None
        for a1, a2, v in zip(p1, p2, vs):
            t1 = _bdot(a1.astype(BF16), v)
            t2 = _bdot(a2.astype(BF16), v)
            o1 = t1 if o1 is None else o1 + t1
            o2 = t2 if o2 is None else o2 + t2
        o = o1 * r1 - o2 * r2
        ms = jnp.mean(o * o, axis=-1, keepdims=True)
        o_ref[:, cols] = o * lax.rsqrt(ms + LN_EPS) * g_ref[...] * (1.0 - lam_init)


def _attention(qkv, lam_params, subln_g, cache_k, cache_v, lam_init):
    g2 = subln_g.reshape(1, HEAD_W)
    small = [pl.BlockSpec((4, HEAD_DIM), lambda *_: (0, 0)), pl.BlockSpec((1, HEAD_W), lambda *_: (0, 0))]
    wc = HEADS_CTX * HEAD_W
    o_ctx = pl.pallas_call(
        functools.partial(_attn_kernel, lam_init=lam_init, has_cache=False, heads=HEADS_CTX),
        grid=(B_P, N_HEADS // HEADS_CTX),
        in_specs=small + [
            pl.BlockSpec((None, SEQ_P, wc), lambda b, h: (0, b, h)),
            pl.BlockSpec((None, SEQ_P, wc), lambda b, h: (1, b, h)),
            pl.BlockSpec((None, SEQ_P, wc), lambda b, h: (2, b, h)),
        ],
        out_specs=pl.BlockSpec((SEQ_P, wc), lambda b, h: (b, h)),
        out_shape=jax.ShapeDtypeStruct((N_P, D), F32),
        compiler_params=_cparams(("parallel", "parallel")),
        name="attn_ctx",
    )(lam_params, g2, qkv, qkv, qkv)
    nq = SEQ_S // TQ
    q0 = N_P // TQ
    s0 = N_P // SEQ_S
    wl = HEADS_LAT * HEAD_W
    o_lat = pl.pallas_call(
        functools.partial(_attn_kernel, lam_init=lam_init, has_cache=True, heads=HEADS_LAT),
        grid=(B_S, N_HEADS // HEADS_LAT, nq),
        in_specs=small + [
            pl.BlockSpec((None, TQ, wl), lambda b, h, t: (0, q0 + b * nq + t, h)),
            pl.BlockSpec((None, SEQ_S, wl), lambda b, h, t: (1, s0 + b, h)),
            pl.BlockSpec((None, SEQ_S, wl), lambda b, h, t: (2, s0 + b, h)),
            pl.BlockSpec((PAST, wl), lambda b, h, t: (b, h)),
            pl.BlockSpec((PAST, wl), lambda b, h, t: (b, h)),
        ],
        out_specs=pl.BlockSpec((TQ, wl), lambda b, h, t: (b * nq + t, h)),
        out_shape=jax.ShapeDtypeStruct((N_S, D), F32),
        compiler_params=_cparams(("parallel", "parallel", "arbitrary")),
        name="attn_lat",
    )(lam_params, g2, qkv, qkv, qkv, cache_k.reshape(B_S * PAST, D), cache_v.reshape(B_S * PAST, D))
    return o_ctx, o_lat


def _router_kernel(x_ref, mod_ref, w_ref, b_ref, h_ref, r_ref, cnt_ref, carry_ref, *, tm):
    @pl.when(pl.program_id(0) == 0)
    def _():
        carry_ref[...] = jnp.zeros_like(carry_ref)

    h = x_ref[...] * (1.0 + mod_ref[4:5, :]) + mod_ref[3:4, :]
    for c in range(SLAB):
        h_ref[pl.ds(c, tm, stride=SLAB), :] = h[:, c * LANES:(c + 1) * LANES]
    hh = h.astype(BF16)
    hl = (h - hh.astype(F32)).astype(BF16)
    w = w_ref[...]
    t = _bdot(hh, w)
    logit = t[:, :LANES] + t[:, LANES:] + _bdot(hl, w[:, :LANES]) + b_ref[...]
    lane = lax.broadcasted_iota(I32, logit.shape, 1).astype(F32)
    neg = -jnp.inf
    big = 4.0 * LANES
    lg = jnp.where(lane < N_GROUPS, logit, neg)
    gmax = lg.max(axis=-1, keepdims=True)
    grp = jnp.where(lg == gmax, lane, big).min(axis=-1, keepdims=True)
    pg = 1.0 / jnp.exp(lg - gmax).sum(axis=-1, keepdims=True)
    lo = N_GROUPS + grp * EPG
    le = jnp.where((lane >= lo) & (lane < lo + EPG), logit, neg)
    v1 = le.max(axis=-1, keepdims=True)
    i1 = jnp.where(le == v1, lane, big).min(axis=-1, keepdims=True)
    le2 = jnp.where(lane == i1, neg, le)
    v2 = le2.max(axis=-1, keepdims=True)
    i2 = jnp.where(le2 == v2, lane, big).min(axis=-1, keepdims=True)
    e = jnp.exp(v2 - v1)
    den = 1.0 / (1.0 + e)
    w1 = pg * den
    w2 = pg * (e * den)
    e1 = i1 - N_GROUPS
    e2 = i2 - N_GROUPS
    r_ref[...] = jnp.where(lane == 0, e1, jnp.where(lane == 1, e2, jnp.where(lane == 2, w1, jnp.where(lane == 3, w2, 0.0))))
    picked = jnp.where(lane == e1, 1.0, jnp.where(lane == e2, 1.0, 0.0))
    carry_ref[...] = carry_ref[...] + picked.sum(axis=0, keepdims=True)
    cnt_ref[...] = jnp.broadcast_to(carry_ref[...], cnt_ref.shape)


def _router(x, mods_l, w_hl, b_r):
    tm = TM_LN
    return pl.pallas_call(
        functools.partial(_router_kernel, tm=tm),
        grid=(N_TOK // tm,),
        in_specs=[
            pl.BlockSpec((tm, D), lambda i: (i, 0)),
            pl.BlockSpec((None, 6, D), lambda i: (_mod_row(i, tm), 0, 0)),
            pl.BlockSpec((D, 2 * LANES), lambda i: (0, 0)),
            pl.BlockSpec((1, LANES), lambda i: (0, 0)),
        ],
        out_specs=[pl.BlockSpec((tm * SLAB, LANES), lambda i: (i, 0)),
                   pl.BlockSpec((tm, LANES), lambda i: (i, 0)),
                   pl.BlockSpec((SUBLANES, LANES), lambda i: (0, 0))],
        out_shape=[jax.ShapeDtypeStruct((N_TOK * SLAB, LANES), F32),
                   jax.ShapeDtypeStruct((N_TOK, LANES), F32),
                   jax.ShapeDtypeStruct((SUBLANES, LANES), F32)],
        scratch_shapes=[pltpu.VMEM((1, LANES), F32)],
        compiler_params=_cparams(("arbitrary",)),
        name="router",
    )(x, mods_l, w_hl, b_r)


def _slot_kernel(r_ref, start_ref, slot_ref, carry_ref):
    @pl.when(pl.program_id(0) == 0)
    def _():
        carry_ref[...] = jnp.zeros_like(carry_ref)

    r = r_ref[...]
    lane = lax.broadcasted_iota(I32, r.shape, 1)
    e1 = r[:, 0:1].astype(I32)
    e2 = r[:, 1:2].astype(I32)
    oh1 = (lane == e1).astype(F32)
    oh2 = (lane == e2).astype(F32)
    oh = oh1 + oh2
    rr = lax.broadcasted_iota(I32, (TR, TR), 0)
    cc = lax.broadcasted_iota(I32, (TR, TR), 1)
    tri = (cc < rr).astype(BF16)
    before = _bdot(tri, oh.astype(BF16)) + carry_ref[...] + start_ref[...]
    k1 = (oh1 * before).sum(axis=-1, keepdims=True)
    k2 = (oh2 * before).sum(axis=-1, keepdims=True)
    slot_ref[...] = jnp.where(lane == 0, k1, jnp.where(lane == 1, k2, 0.0))
    carry_ref[...] = carry_ref[...] + oh.sum(axis=0, keepdims=True)


def _slots(route, start):
    return pl.pallas_call(
        _slot_kernel,
        grid=(N_TOK // TR,),
        in_specs=[pl.BlockSpec((TR, LANES), lambda i: (i, 0)), pl.BlockSpec((1, LANES), lambda i: (0, 0))],
        out_specs=pl.BlockSpec((TR, LANES), lambda i: (i, 0)),
        out_shape=jax.ShapeDtypeStruct((N_TOK, LANES), F32),
        scratch_shapes=[pltpu.VMEM((1, LANES), F32)],
        compiler_params=_cparams(("arbitrary",)),
        name="expert_slots",
    )(route, start)


def _row_copy_waits(n, wait_rows):
    p = TB
    while p >= 1:
        @pl.when((n & p) != 0)
        def _(p=p):
            wait_rows(p)
        p //= 2


def _experts_kernel(cnt_ref, blk0_ref, tok_ref, dst_ref,
                    h_hbm, wg_hbm, wu_hbm, wd_hbm,
                    out_hbm,
                    wg_f, wu_f, wd_f, wg_s, wu_s, wd_s, xbuf, x2d, ybuf, st, wsem, gsem, ssem, *, layer):
    e = pl.program_id(0)
    cnt = cnt_ref[e]
    nblk = (cnt + TB - 1) // TB
    slot0 = blk0_ref[e] * TB

    def weight_copies(ex, wbuf):
        return [pltpu.make_async_copy(src.at[layer, ex], dst.at[wbuf], wsem.at[wbuf, k])
                for k, (src, dst) in enumerate(((wg_hbm, wg_f), (wu_hbm, wu_f), (wd_hbm, wd_f)))]

    @pl.when(e == 0)
    def _():
        for cp in weight_copies(0, 0):
            cp.start(priority=1)

    @pl.when(e + 1 < N_EXPERTS)
    def _():
        for cp in weight_copies(e + 1, (e + 1) % 2):
            cp.start(priority=1)

    @pl.when(e == 0)
    def _():
        xbuf[...] = jnp.zeros_like(xbuf)

    def rows_in(b):
        return jnp.minimum(cnt - b * TB, TB)

    def slab_rows(i, n):
        start = i * SLAB
        return pl.ds(start if isinstance(i, int) else pl.multiple_of(start, SLAB), n * SLAB)

    def gather_desc(buf, r, t, n):
        return pltpu.make_async_copy(h_hbm.at[slab_rows(t, n)], xbuf.at[buf, slab_rows(r, n)], gsem.at[buf])

    def scatter_desc(r, d, n):
        return pltpu.make_async_copy(ybuf.at[slab_rows(r, n)], out_hbm.at[slab_rows(d, n)], ssem)

    def start_gather(base, n_rows, buf):
        def issue(r, carry):
            gather_desc(buf, r, tok_ref[base + r], 1).start()
            return carry

        lax.fori_loop(0, n_rows, issue, 0)

    def start_scatter(base, n_rows):
        def issue(r, carry):
            scatter_desc(r, dst_ref[base + r], 1).start()
            return carry

        lax.fori_loop(0, n_rows, issue, 0)
        st[0] = n_rows

    def wait_scatter():
        _row_copy_waits(st[0], lambda n: scatter_desc(0, 0, n).wait())
        st[0] = 0

    @pl.when(e == 0)
    def _():
        st[0] = 0
        st[1] = 0
        st[2] = 0

    par = st[1]
    nxt = jnp.minimum(e + 1, N_EXPERTS - 1)
    nxt_cnt = jnp.where(e + 1 < N_EXPERTS, cnt_ref[nxt], 0)

    @pl.when((nblk > 0) & (st[2] == 0))
    def _():
        start_gather(slot0, rows_in(0), par)

    wbuf = e % 2
    for cp in weight_copies(e, wbuf):
        cp.wait()

    @pl.when(nblk > 0)
    def _():
        wg_s[...] = wg_f[wbuf].astype(BF16)
        wu_s[...] = wu_f[wbuf].astype(BF16)
        wd_s[...] = wd_f[wbuf].astype(BF16)

        def block(b, carry):
            buf = (par + b) % 2

            @pl.when(b + 1 < nblk)
            def _():
                start_gather(slot0 + (b + 1) * TB, rows_in(b + 1), 1 - buf)

            @pl.when((b + 1 == nblk) & (nxt_cnt > 0))
            def _():
                start_gather(blk0_ref[nxt] * TB, jnp.minimum(nxt_cnt, TB), 1 - buf)

            _row_copy_waits(rows_in(b), lambda n: gather_desc(buf, 0, 0, n).wait())
            for c in range(SLAB):
                x2d[:, c * LANES:(c + 1) * LANES] = xbuf[buf, pl.ds(c, TB, stride=SLAB), :].astype(BF16)
            x = x2d[...]
            g = _bdot(x, wg_s[...])
            u = _bdot(x, wu_s[...])
            y = _bdot((_silu(g) * u).astype(BF16), wd_s[...])
            wait_scatter()
            for c in range(SLAB):
                ybuf[pl.ds(c, TB, stride=SLAB), :] = y[:, c * LANES:(c + 1) * LANES]
            start_scatter(slot0 + b * TB, rows_in(b))
            return carry

        lax.fori_loop(0, nblk, block, 0)
        st[1] = (par + nblk) % 2
        st[2] = jnp.where(nxt_cnt > 0, 1, 0)

    @pl.when(nblk == 0)
    def _():
        st[2] = 0

    @pl.when(e == N_EXPERTS - 1)
    def _():
        wait_scatter()


def _experts(h_slab, cnt, blk0, slot_tok, slot_dst, w_gate, w_up, w_down, layer):
    hbm = pl.BlockSpec(memory_space=pl.ANY)
    return pl.pallas_call(
        functools.partial(_experts_kernel, layer=layer),
        grid_spec=pltpu.PrefetchScalarGridSpec(
            num_scalar_prefetch=4,
            grid=(N_EXPERTS,),
            in_specs=[hbm, hbm, hbm, hbm],
            out_specs=hbm,
            scratch_shapes=[pltpu.VMEM((2, D, D_EXPERT), F32), pltpu.VMEM((2, D, D_EXPERT), F32),
                            pltpu.VMEM((2, D_EXPERT, D), F32),
                            pltpu.VMEM((D, D_EXPERT), BF16), pltpu.VMEM((D, D_EXPERT), BF16),
                            pltpu.VMEM((D_EXPERT, D), BF16),
                            pltpu.VMEM((2, TB * SLAB, LANES), F32),
                            pltpu.VMEM((TB, D), BF16),
                            pltpu.VMEM((TB * SLAB, LANES), F32),
                            pltpu.SMEM((4,), I32),
                            pltpu.SemaphoreType.DMA((2, 3)),
                            pltpu.SemaphoreType.DMA((2,)),
                            pltpu.SemaphoreType.DMA(())],
        ),
        out_shape=jax.ShapeDtypeStruct((TOP_K * N_TOK * SLAB, LANES), F32),
        compiler_params=_cparams(("arbitrary",)),
        name="experts",
    )(cnt, blk0, slot_tok, slot_dst, h_slab, w_gate, w_up, w_down)


def _ffn_out_kernel(x_ref, r0_ref, r1_ref, route_ref, mod_ref, lg_ref, lb_ref, o_ref, *, tm):
    rt = route_ref[...]
    w0, w1 = rt[:, 2:3], rt[:, 3:4]
    f = jnp.concatenate(
        [r0_ref[pl.ds(c, tm, stride=SLAB), :] * w0 + r1_ref[pl.ds(c, tm, stride=SLAB), :] * w1
         for c in range(SLAB)], axis=1)
    z = ALPHA * x_ref[...] + mod_ref[5:6, :] * f
    o_ref[...] = _ln(z, lg_ref[...], lb_ref[...])


def _ffn_out(x, picked, route, mods_l, ln_g, ln_b):
    tm = TM_LN
    nb = N_TOK // tm
    row, vec, mod = _row_specs(tm)
    return pl.pallas_call(
        functools.partial(_ffn_out_kernel, tm=tm),
        grid=(nb,),
        in_specs=[row,
                  pl.BlockSpec((tm * SLAB, LANES), lambda i: (i, 0)),
                  pl.BlockSpec((tm * SLAB, LANES), lambda i: (i + nb, 0)),
                  pl.BlockSpec((tm, LANES), lambda i: (i, 0)),
                  mod, vec, vec],
        out_specs=row,
        out_shape=jax.ShapeDtypeStruct((N_TOK, D), F32),
        compiler_params=_cparams(("parallel",)),
        name="ffn_out",
    )(x, picked, picked, route, mods_l, ln_g.reshape(1, D), ln_b.reshape(1, D))


def _moe(x, mods_l, layer, w_rg, b_rg, w_re, b_re, w_gate, w_up, w_down, ln_g, ln_b):
    w_r = jnp.zeros((D, LANES), F32).at[:, :N_GROUPS].set(w_rg).at[:, N_GROUPS:N_GROUPS + N_EXPERTS].set(w_re)
    w_hi = w_r.astype(BF16)
    w_lo = (w_r - w_hi.astype(F32)).astype(BF16)
    b_r = jnp.zeros((1, LANES), F32).at[0, :N_GROUPS].set(b_rg).at[0, N_GROUPS:N_GROUPS + N_EXPERTS].set(b_re)
    h_slab, route, counts = _router(x, mods_l, jnp.concatenate([w_hi, w_lo], axis=1), b_r)
    cnt = counts[0, :N_EXPERTS].astype(I32)
    nblk = (cnt + TB - 1) // TB
    blk0 = jnp.cumsum(nblk) - nblk
    start = jnp.zeros((1, LANES), F32).at[0, :N_EXPERTS].set((blk0 * TB).astype(F32))
    slot = _slots(route, start)[:, :TOP_K].astype(I32)
    dst = jnp.arange(TOP_K, dtype=I32)[None, :] * N_TOK + jnp.arange(N_TOK, dtype=I32)[:, None]
    slot_dst = jnp.zeros((N_SLOTS,), I32).at[slot.reshape(-1)].set(dst.reshape(-1))
    slot_tok = slot_dst % N_TOK
    picked = _experts(h_slab, cnt, blk0.astype(I32), slot_tok, slot_dst, w_gate, w_up, w_down, layer)
    return _ffn_out(x, picked, route, mods_l, ln_g, ln_b)


def kernel(x_prompt, x_sample, c, cache_k_l1, cache_v_l1, cache_k_l3, cache_v_l3, c_ctx, w_mod, b_mod, ln_mix_g, ln_mix_b, ln_ffn_g, ln_ffn_b, conv_w_pw1, conv_b_pw1, conv_w_dw, conv_b_dw, conv_ln_g, conv_ln_b, conv_w_pw2, conv_b_pw2, attn_w_q, attn_w_k, attn_w_v, attn_w_o, attn_lambda_q1, attn_lambda_k1, attn_lambda_q2, attn_lambda_k2, attn_subln_g, router_w_group, router_b_group, router_w_expert, router_b_expert, moe_w_gate, moe_w_up, moe_w_down):
    caches = ((cache_k_l1, cache_v_l1), (cache_k_l3, cache_v_l3))
    x = jnp.concatenate([x_prompt.reshape(N_P, D), x_sample.reshape(N_S, D)], axis=0)
    cond = jnp.zeros((N_MOD_ROWS, D), F32).at[0].set(c_ctx).at[1:1 + B_S].set(c)
    mods = _modulation(cond, w_mod, b_mod).reshape(DEPTH, N_MOD_ROWS, 6, D)
    rope = _rope_tables()
    new_kv = []
    for i in range(DEPTH):
        j = i // 2
        mods_l = mods[i]
        if i % 2 == 0:
            u = _pw1_glu(x, mods_l, conv_w_pw1[j].astype(BF16), conv_b_pw1[j])
            u = _dwconv(u, conv_w_dw[j], conv_b_dw[j])
            x = _conv_out(u, x, mods_l, conv_w_pw2[j].astype(BF16), conv_b_pw2[j], conv_ln_g[j], conv_ln_b[j],
                          ln_mix_g[i], ln_mix_b[i])
        else:
            lam_init = 0.8 - 0.6 * math.exp(-0.3 * i)
            w_qkv = jnp.stack([attn_w_q[j], attn_w_k[j], attn_w_v[j]]).astype(BF16)
            qkv = _qkv_proj(x, mods_l, w_qkv, rope)
            new_kv.append(qkv[1, :N_P].reshape(B_P, SEQ_P, N_HEADS, HEAD_W))
            new_kv.append(qkv[2, :N_P].reshape(B_P, SEQ_P, N_HEADS, HEAD_W))
            lam_params = jnp.stack([attn_lambda_q1[j], attn_lambda_k1[j], attn_lambda_q2[j], attn_lambda_k2[j]])
            o_ctx, o_lat = _attention(qkv, lam_params, attn_subln_g[j], caches[j][0], caches[j][1], lam_init)
            x = _attn_out(o_ctx, o_lat, x, mods_l, attn_w_o[j].astype(BF16), ln_mix_g[i], ln_mix_b[i])
        x = _moe(x, mods_l, i, router_w_group[i], router_b_group[i], router_w_expert[i], router_b_expert[i],
                 moe_w_gate, moe_w_up, moe_w_down, ln_ffn_g[i], ln_ffn_b[i])
    return (x[:N_P].reshape(B_P, SEQ_P, D), x[N_P:].reshape(B_S, SEQ_S, D),
            new_kv[0], new_kv[1], new_kv[2], new_kv[3])
```
